```python
import math
import jax
import jax.numpy as jnp
from jax import lax
import numpy as np

D_MODEL = 4096
BATCH = 4
SEQ = 2048
DEPTH = 2
DEC_BATCH = 128
DEC_SEQ = 8
PAST_LEN = 16384
PAGE_SIZE = 128

N_BRANCH = 4
BRANCH_W = D_MODEL // N_BRANCH
FOX_HEADS = 8
FOX_HEAD_DIM = BRANCH_W // FOX_HEADS
FOX_KV_HEADS = 1
FOX_FORGET_BIAS = 3.0
GLA_HEADS = 4
GLA_DK = BRANCH_W // 2 // GLA_HEADS
GLA_DV = BRANCH_W // GLA_HEADS
GLA_GATE_RANK = 16
GLA_TAU = 16.0
GLA_CHUNK = 64
SSD_HEAD_DIM = 64
SSD_HEADS = BRANCH_W // SSD_HEAD_DIM
SSD_STATE = 128
SSD_GROUPS = 4
SSD_CONV = 4
SSD_CHUNK = 128
SSD_CONV_CH = BRANCH_W + 2 * SSD_GROUPS * SSD_STATE
MLA_HEADS = 8
MLA_NOPE = 128
MLA_ROPE = 32
MLA_V = BRANCH_W // MLA_HEADS
MLA_Q_RANK = 768
MLA_KV_RANK = 256
MLA_SCALE = (MLA_NOPE + MLA_ROPE) ** -0.5
ROPE_THETA = 10000.0
N_EXPERTS = 64
TOP_K = 8
EXPERT_DIM = 256
SHARED_DIM = 1024
N_ROUTE_GROUPS = 8
TOPK_ROUTE_GROUPS = 4
ROUTED_SCALE = 2.5

Q_BLOCK = 128
EPS = 1e-6
F32 = jnp.float32

IN_SPLITS = (FOX_HEADS * FOX_HEAD_DIM, FOX_KV_HEADS * FOX_HEAD_DIM, FOX_KV_HEADS * FOX_HEAD_DIM, FOX_HEADS,
             GLA_HEADS * GLA_DK, GLA_HEADS * GLA_DK, GLA_HEADS * GLA_DV, GLA_HEADS * GLA_DV, GLA_GATE_RANK,
             BRANCH_W, SSD_CONV_CH, SSD_HEADS,
             MLA_Q_RANK, MLA_KV_RANK, MLA_ROPE,
             N_BRANCH * D_MODEL)
N_IN = sum(IN_SPLITS)
IN_OFFSETS = tuple(int(v) for v in np.cumsum(IN_SPLITS)[:-1])

kernel_name = 'hybrid_fox_gla_ssd_mla_moe_adaln_step'


def rmsnorm(x, g):
    xf = x.astype(F32)
    y = xf * lax.rsqrt(jnp.mean(xf * xf, axis=-1, keepdims=True) + EPS)
    return (y * g.astype(F32)).astype(x.dtype)


def rope(x, pos):
    half = x.shape[-1] // 2
    freqs = ROPE_THETA ** (-jnp.arange(half, dtype=F32) / half)
    ang = pos.astype(F32)[:, None] * freqs[None, :]
    ang = ang.reshape((1, pos.shape[0]) + (1,) * (x.ndim - 3) + (half,))
    cos, sin = jnp.cos(ang), jnp.sin(ang)
    xf = x.astype(F32)
    x1, x2 = xf[..., :half], xf[..., half:]
    return jnp.concatenate([x1 * cos - x2 * sin, x2 * cos + x1 * sin], axis=-1).astype(x.dtype)


def two_part_softmax(s_a, s_b):
    m = jnp.maximum(s_a.max(-1, keepdims=True), s_b.max(-1, keepdims=True))
    e_a = jnp.exp(s_a - m)
    e_b = jnp.exp(s_b - m)
    denom = e_a.sum(-1, keepdims=True) + e_b.sum(-1, keepdims=True)
    return e_a / denom, e_b / denom


def gather_pages(pool, l, page_table):
    rows = pool[l, page_table]
    return rows.reshape((page_table.shape[0], page_table.shape[1] * pool.shape[2]) + pool.shape[3:])


def fox_prompt(q, k, v, logf):
    B, S = q.shape[:2]
    n, g = FOX_KV_HEADS, FOX_HEADS // FOX_KV_HEADS
    nb = S // Q_BLOCK
    cum = jnp.cumsum(logf, axis=1)
    qb = jnp.moveaxis(q.reshape(B, nb, Q_BLOCK, n, g, FOX_HEAD_DIM), 1, 0)
    cb = jnp.moveaxis(cum.reshape(B, nb, Q_BLOCK, n, g), 1, 0)
    ck = cum.reshape(B, S, n, g).transpose(0, 2, 3, 1)[:, :, :, None, :]
    kpos = jnp.arange(S)
    scale = FOX_HEAD_DIM ** -0.5

    def block(args):
        qi, ci, i = args
        s = jnp.einsum('bqngd,bknd->bngqk', qi, k, preferred_element_type=F32) * scale
        s = s + ci.transpose(0, 2, 3, 1)[..., None] - ck
        qpos = i * Q_BLOCK + jnp.arange(Q_BLOCK)
        s = jnp.where(kpos[None, :] <= qpos[:, None], s, -jnp.inf)
        p = jax.nn.softmax(s, axis=-1)
        return jnp.einsum('bngqk,bknd->bqngd', p.astype(v.dtype), v)

    o = lax.map(block, (qb, cb, jnp.arange(nb)))
    return jnp.moveaxis(o, 0, 1).reshape(B, S, FOX_HEADS * FOX_HEAD_DIM)


def fox_sample(q, k, v, logf, k_past, v_past, logf_past):
    B, L = q.shape[:2]
    n, g = FOX_KV_HEADS, FOX_HEADS // FOX_KV_HEADS
    scale = FOX_HEAD_DIM ** -0.5
    qg = q.reshape(B, L, n, g, FOX_HEAD_DIM)
    suffix = lax.cumsum(logf_past, axis=1, reverse=True) - logf_past
    suffix = suffix.reshape(B, -1, n, g).transpose(0, 2, 3, 1)
    cum = jnp.cumsum(logf, axis=1).reshape(B, L, n, g).transpose(0, 2, 3, 1)
    s_past = jnp.einsum('bqngd,bknd->bngqk', qg, k_past, preferred_element_type=F32) * scale
    s_past = s_past + cum[..., :, None] + suffix[..., None, :]
    s_new = jnp.einsum('bqngd,bknd->bngqk', qg, k, preferred_element_type=F32) * scale
    s_new = s_new + cum[..., :, None] - cum[..., None, :]
    s_new = jnp.where(jnp.tril(jnp.ones((L, L), bool)), s_new, -jnp.inf)
    p_past, p_new = two_part_softmax(s_past, s_new)
    o = (jnp.einsum('bngqk,bknd->bqngd', p_past.astype(v.dtype), v_past)
         + jnp.einsum('bngqk,bknd->bqngd', p_new.astype(v.dtype), v))
    return o.reshape(B, L, FOX_HEADS * FOX_HEAD_DIM)


def gla_chunked(q, k, v, log_a, S0):
    B, L = q.shape[:2]
    C = math.gcd(L, GLA_CHUNK)
    nc = L // C
    ch = lambda t: t.reshape((B, nc, C) + t.shape[2:])
    qc = ch(q.astype(F32) * GLA_DK ** -0.5)
    kc, vc, gc = ch(k.astype(F32)), ch(v.astype(F32)), ch(log_a)
    b = jnp.cumsum(gc, axis=2)
    b_last = b[:, :, -1:]
    q_hat = qc * jnp.exp(b)
    k_hat = kc * jnp.exp(-b)
    k_tail = kc * jnp.exp(b_last - b)
    A = jnp.einsum('bctha,bcsha->bchts', q_hat, k_hat)
    A = jnp.where(jnp.tril(jnp.ones((C, C), bool)), A, 0.0)
    o_intra = jnp.einsum('bchts,bcshv->bcthv', A, vc)
    kv_chunk = jnp.einsum('bcsha,bcshv->bchav', k_tail, vc)
    decay = jnp.exp(b_last[:, :, 0])

    def step(S, xs):
        qh, kv, d = xs
        o = jnp.einsum('btha,bhav->bthv', qh, S)
        return S * d[..., None] + kv, o

    S_fin, o_inter = lax.scan(step, S0, (jnp.moveaxis(q_hat, 1, 0), jnp.moveaxis(kv_chunk, 1, 0),
                                         jnp.moveaxis(decay, 1, 0)))
    o = o_intra + jnp.moveaxis(o_inter, 0, 1)
    return o.reshape(B, L, GLA_HEADS, GLA_DV), S_fin


def causal_conv(x, buf, w, b):
    full = jnp.concatenate([buf, x], axis=1)
    out = lax.conv_general_dilated(full, w[:, None, :], window_strides=(1,), padding='VALID',
                                   dimension_numbers=('NWC', 'WIO', 'NWC'),
                                   feature_group_count=x.shape[-1])
    return out + b, full[:, full.shape[1] - (SSD_CONV - 1):]


def ssd_chunked(x, dt, A, Bm, Cm, h0):
    Bz, L = x.shape[:2]
    C = math.gcd(L, SSD_CHUNK)
    nc = L // C
    ch = lambda t: t.reshape((Bz, nc, C) + t.shape[2:])
    xc, dtc = ch(x.astype(F32)), ch(dt)
    Bc, Cc = ch(Bm.astype(F32)), ch(Cm.astype(F32))
    cum = jnp.cumsum(dtc * A, axis=2)
    causal = jnp.tril(jnp.ones((C, C), bool))[None, None, :, :, None, None]
    seg = cum[:, :, :, None] - cum[:, :, None, :]
    decay = jnp.exp(jnp.where(causal, seg, -jnp.inf))
    cb = jnp.einsum('bctgn,bcsgn->bctsg', Cc, Bc)
    M = cb[..., None] * decay * dtc[:, :, None]
    y_intra = jnp.einsum('bctsgj,bcsgjp->bctgjp', M, xc)
    w_s = jnp.exp(cum[:, :, -1:] - cum) * dtc
    dstate = jnp.einsum('bcsgjp,bcsgn->bcgjpn', w_s[..., None] * xc, Bc)
    chunk_decay = jnp.exp(cum[:, :, -1])
    q_decay = jnp.exp(cum)

    def step(h, xs):
        Cq, qd, ds, cd = xs
        y = jnp.einsum('btgn,bgjpn->btgjp', Cq, h) * qd[..., None]
        return h * cd[..., None, None] + ds, y

    h_fin, y_inter = lax.scan(step, h0, (jnp.moveaxis(Cc, 1, 0), jnp.moveaxis(q_decay, 1, 0),
                                         jnp.moveaxis(dstate, 1, 0), jnp.moveaxis(chunk_decay, 1, 0)))
    y = y_intra + jnp.moveaxis(y_inter, 0, 1)
    return y.reshape(x.shape), h_fin


def mla_prompt(q_lat, q_rope, lat, kr):
    B, S = q_lat.shape[:2]
    nb = S // Q_BLOCK
    qlb = jnp.moveaxis(q_lat.reshape((B, nb, Q_BLOCK) + q_lat.shape[2:]), 1, 0)
    qrb = jnp.moveaxis(q_rope.reshape((B, nb, Q_BLOCK) + q_rope.shape[2:]), 1, 0)
    kpos = jnp.arange(S)

    def block(args):
        ql, qr, i = args
        s = (jnp.einsum('bqhr,bkr->bhqk', ql, lat, preferred_element_type=F32)
             + jnp.einsum('bqhe,bke->bhqk', qr, kr, preferred_element_type=F32)) * MLA_SCALE
        qpos = i * Q_BLOCK + jnp.arange(Q_BLOCK)
        s = jnp.where(kpos[None, :] <= qpos[:, None], s, -jnp.inf)
        p = jax.nn.softmax(s, axis=-1)
        return jnp.einsum('bhqk,bkr->bqhr', p.astype(lat.dtype), lat)

    o = lax.map(block, (qlb, qrb, jnp.arange(nb)))
    return jnp.moveaxis(o, 0, 1).reshape(B, S, MLA_HEADS, MLA_KV_RANK)


def mla_sample(q_lat, q_rope, lat, kr, lat_past, kr_past):
    L = q_lat.shape[1]
    s_past = (jnp.einsum('bqhr,bkr->bhqk', q_lat, lat_past, preferred_element_type=F32)
              + jnp.einsum('bqhe,bke->bhqk', q_rope, kr_past, preferred_element_type=F32)) * MLA_SCALE
    s_new = (jnp.einsum('bqhr,bkr->bhqk', q_lat, lat, preferred_element_type=F32)
             + jnp.einsum('bqhe,bke->bhqk', q_rope, kr, preferred_element_type=F32)) * MLA_SCALE
    s_new = jnp.where(jnp.tril(jnp.ones((L, L), bool)), s_new, -jnp.inf)
    p_past, p_new = two_part_softmax(s_past, s_new)
    return (jnp.einsum('bhqk,bkr->bqhr', p_past.astype(lat.dtype), lat_past)
            + jnp.einsum('bhqk,bkr->bqhr', p_new.astype(lat.dtype), lat))


def moe_ffn(x, lw):
    T = x.shape[0]
    s = jax.nn.sigmoid((x @ lw['w_router']).astype(F32))
    sb = s + lw['router_bias'].astype(F32)
    grp = sb.reshape(T, N_ROUTE_GROUPS, N_EXPERTS // N_ROUTE_GROUPS)
    gscore = lax.top_k(grp, 2)[0].sum(-1)
    _, gidx = lax.top_k(gscore, TOPK_ROUTE_GROUPS)
    gmask = jax.nn.one_hot(gidx, N_ROUTE_GROUPS, dtype=F32).sum(-2) > 0
    emask = jnp.repeat(gmask, N_EXPERTS // N_ROUTE_GROUPS, axis=-1)
    _, eidx = lax.top_k(jnp.where(emask, sb, -jnp.inf), TOP_K)
    wts = jnp.take_along_axis(s, eidx, axis=-1)
    wts = wts / wts.sum(-1, keepdims=True) * ROUTED_SCALE
    combine = jnp.einsum('tk,tke->te', wts, jax.nn.one_hot(eidx, N_EXPERTS, dtype=F32))
    hg = jnp.einsum('td,edf->tef', x, lw['w_exp_gate'])
    hu = jnp.einsum('td,edf->tef', x, lw['w_exp_up'])
    hidden = jax.nn.silu(hg) * hu * combine[..., None].astype(x.dtype)
    routed = jnp.einsum('tef,efd->td', hidden, lw['w_exp_down'])
    shared = (jax.nn.silu(x @ lw['w_sh_gate']) * (x @ lw['w_sh_up'])) @ lw['w_sh_down']
    return routed + shared


def trunk_layer(h, c, pos, lw, past):
    B, L, _ = h.shape
    dt_ = h.dtype
    mod = (jax.nn.silu(c) @ lw['w_ada'] + lw['b_ada']).reshape(B, 6, 1, D_MODEL)
    shift1, scale1, gate1, shift2, scale2, gate2 = (mod[:, i] for i in range(6))
    u = rmsnorm(h, lw['norm1']) * (1 + scale1) + shift1
    (f_q, f_k, f_v, f_f, g_q, g_k, g_v, g_o, g_a, s_z, s_xbc, s_dt,
     m_cq, m_ckv, m_kr, br_gate) = jnp.split(u @ lw['w_in'], IN_OFFSETS, axis=-1)

    fq = f_q.reshape(B, L, FOX_HEADS, FOX_HEAD_DIM)
    fk = f_k.reshape(B, L, FOX_KV_HEADS, FOX_HEAD_DIM)
    fv = f_v.reshape(B, L, FOX_KV_HEADS, FOX_HEAD_DIM)
    flogf = jax.nn.log_sigmoid(f_f.astype(F32) + lw['fox_fb'].astype(F32))
    if past is None:
        o_fox = fox_prompt(fq, fk, fv, flogf)
    else:
        o_fox = fox_sample(fq, fk, fv, flogf, past['fox_k'], past['fox_v'], past['fox_logf'].astype(F32))

    gq = g_q.reshape(B, L, GLA_HEADS, GLA_DK)
    gk = g_k.reshape(B, L, GLA_HEADS, GLA_DK)
    gv = g_v.reshape(B, L, GLA_HEADS, GLA_DV)
    log_a = jax.nn.log_sigmoid((g_a @ lw['gla_w_a2'] + lw['gla_b_a']).astype(F32)) / GLA_TAU
    log_a = log_a.reshape(B, L, GLA_HEADS, GLA_DK)
    S0 = jnp.zeros((B, GLA_HEADS, GLA_DK, GLA_DV), F32) if past is None else past['gla'].astype(F32)
    o_g, S_new = gla_chunked(gq, gk, gv, log_a, S0)
    o_gla = rmsnorm(o_g, lw['gla_norm']).astype(dt_).reshape(B, L, BRANCH_W) * jax.nn.silu(g_o)

    buf = jnp.zeros((B, SSD_CONV - 1, SSD_CONV_CH), dt_) if past is None else past['conv'].astype(dt_)
    xbc, buf_new = causal_conv(s_xbc, buf, lw['ssd_conv_w'], lw['ssd_conv_b'])
    xbc = jax.nn.silu(xbc)
    s_x, s_B, s_C = jnp.split(xbc, [BRANCH_W, BRANCH_W + SSD_GROUPS * SSD_STATE], axis=-1)
    J = SSD_HEADS // SSD_GROUPS
    dt = jax.nn.softplus(s_dt.astype(F32) + lw['ssd_dt_bias'].astype(F32))
    A = -jnp.exp(lw['ssd_A_log'].astype(F32))
    h0 = jnp.zeros((B, SSD_HEADS, SSD_HEAD_DIM, SSD_STATE), F32) if past is None else past['ssd'].astype(F32)
    xs = s_x.reshape(B, L, SSD_GROUPS, J, SSD_HEAD_DIM)
    y, h_new = ssd_chunked(xs, dt.reshape(B, L, SSD_GROUPS, J), A.reshape(SSD_GROUPS, J),
                           s_B.reshape(B, L, SSD_GROUPS, SSD_STATE), s_C.reshape(B, L, SSD_GROUPS, SSD_STATE),
                           h0.reshape(B, SSD_GROUPS, J, SSD_HEAD_DIM, SSD_STATE))
    y = y + lw['ssd_D'].astype(F32).reshape(SSD_GROUPS, J, 1) * xs.astype(F32)
    y = y.reshape(B, L, BRANCH_W) * jax.nn.silu(s_z.astype(F32))
    gw = BRANCH_W // SSD_GROUPS
    o_ssd = rmsnorm(y.reshape(B, L, SSD_GROUPS, gw), lw['ssd_norm'].reshape(SSD_GROUPS, gw))
    o_ssd = o_ssd.reshape(B, L, BRANCH_W).astype(dt_)
    h_new = h_new.reshape(B, SSD_HEADS, SSD_HEAD_DIM, SSD_STATE)

    cq = rmsnorm(m_cq, lw['mla_q_norm'])
    q = jnp.einsum('blc,che->blhe', cq, lw['mla_w_uq'])
    q_nope, q_rope = q[..., :MLA_NOPE], rope(q[..., MLA_NOPE:], pos)
    lat = rmsnorm(m_ckv, lw['mla_kv_norm'])
    kr = rope(m_kr, pos)
    q_lat = jnp.einsum('blhd,rhd->blhr', q_nope, lw['mla_w_uk'])
    if past is None:
        o_lat = mla_prompt(q_lat, q_rope, lat, kr)
    else:
        o_lat = mla_sample(q_lat, q_rope, lat, kr, past['mla_lat'], past['mla_kr'])
    o_mla = jnp.einsum('blhr,rhv->blhv', o_lat, lw['mla_w_uv']).reshape(B, L, BRANCH_W)

    branches = jnp.stack([o_fox, o_gla, o_ssd, o_mla], axis=2)
    gates = jax.nn.sigmoid(br_gate.reshape(B, L, N_BRANCH, D_MODEL))
    merged = (gates * jnp.einsum('blnw,nwd->blnd', branches, lw['w_branch'])).sum(axis=2)
    h = h + gate1 * (merged @ lw['w_out'])

    u2 = rmsnorm(h, lw['norm2']) * (1 + scale2) + shift2
    h = h + gate2 * moe_ffn(u2.reshape(B * L, D_MODEL), lw).reshape(B, L, D_MODEL)
    new = (fk, fv, flogf.astype(dt_), lat, kr, S_new.astype(dt_), h_new.astype(dt_), buf_new)
    return h, new


def setup_inputs(seed: int = 0) -> dict:
    key = jax.random.key(seed)
    keys = jax.random.split(key, 64)
    ctr = [0]

    def nk():
        ctr[0] += 1
        return keys[ctr[0] - 1]

    def nrm(shape, s):
        return jax.random.normal(nk(), shape, F32) * s

    def gain(shape):
        return 1.0 + 0.02 * jax.random.normal(nk(), shape, F32)

    D = D_MODEL
    n_pages = PAST_LEN // PAGE_SIZE
    n_pool = (DEC_BATCH * n_pages * 5) // 4
    inp = {}
    inp['x_prompt'] = nrm((BATCH, SEQ, D), 1.0)
    inp['x_sample'] = nrm((DEC_BATCH, DEC_SEQ, D), 1.0)
    inp['cache_fox_k'] = nrm((DEPTH, n_pool, PAGE_SIZE, FOX_KV_HEADS, FOX_HEAD_DIM), 1.0)
    inp['cache_fox_v'] = nrm((DEPTH, n_pool, PAGE_SIZE, FOX_KV_HEADS, FOX_HEAD_DIM), 1.0)
    inp['cache_fox_logf'] = jax.nn.log_sigmoid(nrm((DEPTH, n_pool, PAGE_SIZE, FOX_HEADS), 1.0) + FOX_FORGET_BIAS)
    inp['cache_mla_latent'] = nrm((DEPTH, n_pool, PAGE_SIZE, MLA_KV_RANK), 1.0)
    inp['cache_mla_krope'] = nrm((DEPTH, n_pool, PAGE_SIZE, MLA_ROPE), 1.0)
    inp['state_gla'] = nrm((DEPTH, DEC_BATCH, GLA_HEADS, GLA_DK, GLA_DV), 1.0)
    inp['state_ssd'] = nrm((DEPTH, DEC_BATCH, SSD_HEADS, SSD_HEAD_DIM, SSD_STATE), 0.1)
    inp['state_conv'] = nrm((DEPTH, DEC_BATCH, SSD_CONV - 1, SSD_CONV_CH), 1.0)
    perm = jax.random.permutation(nk(), n_pool)[:DEC_BATCH * n_pages]
    inp['page_table'] = perm.reshape(DEC_BATCH, n_pages).astype(jnp.int32)
    inp['c_prompt'] = nrm((BATCH, D), 1.0)
    inp['c_sample'] = nrm((DEC_BATCH, D), 1.0)
    inp['norm1'] = gain((DEPTH, D))
    inp['w_ada'] = nrm((DEPTH, D, 6 * D), 0.5 * D ** -0.5)
    inp['b_ada'] = nrm((DEPTH, 6 * D), 0.02)
    inp['w_in'] = nrm((DEPTH, D, N_IN), D ** -0.5)
    inp['fox_fb'] = FOX_FORGET_BIAS + nrm((DEPTH, FOX_HEADS), 0.1)
    inp['gla_w_a2'] = nrm((DEPTH, GLA_GATE_RANK, GLA_HEADS * GLA_DK), GLA_GATE_RANK ** -0.5)
    inp['gla_b_a'] = nrm((DEPTH, GLA_HEADS * GLA_DK), 0.02)
    inp['gla_norm'] = gain((DEPTH, GLA_DV))
    inp['ssd_conv_w'] = nrm((DEPTH, SSD_CONV, SSD_CONV_CH), SSD_CONV ** -0.5)
    inp['ssd_conv_b'] = nrm((DEPTH, SSD_CONV_CH), 0.02)
    dt0 = jnp.exp(jax.random.uniform(nk(), (DEPTH, SSD_HEADS), F32) * (math.log(0.1) - math.log(0.001))
                  + math.log(0.001))
    inp['ssd_dt_bias'] = dt0 + jnp.log(-jnp.expm1(-dt0))
    inp['ssd_A_log'] = jnp.log(jax.random.uniform(nk(), (DEPTH, SSD_HEADS), F32, 1.0, 16.0))
    inp['ssd_D'] = gain((DEPTH, SSD_HEADS))
    inp['ssd_norm'] = gain((DEPTH, BRANCH_W))
    inp['mla_q_norm'] = gain((DEPTH, MLA_Q_RANK))
    inp['mla_w_uq'] = nrm((DEPTH, MLA_Q_RANK, MLA_HEADS, MLA_NOPE + MLA_ROPE), MLA_Q_RANK ** -0.5)
    inp['mla_kv_norm'] = gain((DEPTH, MLA_KV_RANK))
    inp['mla_w_uk'] = nrm((DEPTH, MLA_KV_RANK, MLA_HEADS, MLA_NOPE), MLA_KV_RANK ** -0.5)
    inp['mla_w_uv'] = nrm((DEPTH, MLA_KV_RANK, MLA_HEADS, MLA_V), MLA_KV_RANK ** -0.5)
    inp['w_branch'] = nrm((DEPTH, N_BRANCH, BRANCH_W, D), BRANCH_W ** -0.5)
    inp['w_out'] = nrm((DEPTH, D, D), D ** -0.5)
    inp['norm2'] = gain((DEPTH, D))
    inp['w_router'] = nrm((DEPTH, D, N_EXPERTS), D ** -0.5)
    inp['router_bias'] = nrm((DEPTH, N_EXPERTS), 0.01)
    inp['w_exp_gate'] = nrm((DEPTH, N_EXPERTS, D, EXPERT_DIM), D ** -0.5)
    inp['w_exp_up'] = nrm((DEPTH, N_EXPERTS, D, EXPERT_DIM), D ** -0.5)
    inp['w_exp_down'] = nrm((DEPTH, N_EXPERTS, EXPERT_DIM, D), EXPERT_DIM ** -0.5)
    inp['w_sh_gate'] = nrm((DEPTH, D, SHARED_DIM), D ** -0.5)
    inp['w_sh_up'] = nrm((DEPTH, D, SHARED_DIM), D ** -0.5)
    inp['w_sh_down'] = nrm((DEPTH, SHARED_DIM, D), SHARED_DIM ** -0.5)
    inp['final_norm'] = gain((D,))
    return inp


def reference(x_prompt, x_sample, cache_fox_k, cache_fox_v, cache_fox_logf, cache_mla_latent, cache_mla_krope,
              state_gla, state_ssd, state_conv, page_table, c_prompt, c_sample,
              norm1, w_ada, b_ada, w_in, fox_fb, gla_w_a2, gla_b_a, gla_norm,
              ssd_conv_w, ssd_conv_b, ssd_dt_bias, ssd_A_log, ssd_D, ssd_norm,
              mla_q_norm, mla_w_uq, mla_kv_norm, mla_w_uk, mla_w_uv,
              w_branch, w_out, norm2, w_router, router_bias, w_exp_gate, w_exp_up, w_exp_down,
              w_sh_gate, w_sh_up, w_sh_down, final_norm):
    past_len = page_table.shape[1] * PAGE_SIZE
    pos_p = jnp.arange(x_prompt.shape[1])
    pos_s = past_len + jnp.arange(x_sample.shape[1])
    h_p, h_s = x_prompt, x_sample
    new_p, new_s = [], []
    for l in range(DEPTH):
        lw = dict(norm1=norm1[l], w_ada=w_ada[l], b_ada=b_ada[l], w_in=w_in[l], fox_fb=fox_fb[l],
                  gla_w_a2=gla_w_a2[l], gla_b_a=gla_b_a[l], gla_norm=gla_norm[l],
                  ssd_conv_w=ssd_conv_w[l], ssd_conv_b=ssd_conv_b[l], ssd_dt_bias=ssd_dt_bias[l],
                  ssd_A_log=ssd_A_log[l], ssd_D=ssd_D[l], ssd_norm=ssd_norm[l],
                  mla_q_norm=mla_q_norm[l], mla_w_uq=mla_w_uq[l], mla_kv_norm=mla_kv_norm[l],
                  mla_w_uk=mla_w_uk[l], mla_w_uv=mla_w_uv[l], w_branch=w_branch[l], w_out=w_out[l],
                  norm2=norm2[l], w_router=w_router[l], router_bias=router_bias[l],
                  w_exp_gate=w_exp_gate[l], w_exp_up=w_exp_up[l], w_exp_down=w_exp_down[l],
                  w_sh_gate=w_sh_gate[l], w_sh_up=w_sh_up[l], w_sh_down=w_sh_down[l])
        h_p, st_p = trunk_layer(h_p, c_prompt, pos_p, lw, None)
        past = dict(fox_k=gather_pages(cache_fox_k, l, page_table),
                    fox_v=gather_pages(cache_fox_v, l, page_table),
                    fox_logf=gather_pages(cache_fox_logf, l, page_table),
                    mla_lat=gather_pages(cache_mla_latent, l, page_table),
                    mla_kr=gather_pages(cache_mla_krope, l, page_table),
                    gla=state_gla[l], ssd=state_ssd[l], conv=state_conv[l])
        h_s, st_s = trunk_layer(h_s, c_sample, pos_s, lw, past)
        new_p.append(st_p)
        new_s.append(st_s)
    y_prompt = rmsnorm(h_p, final_norm)
    y_sample = rmsnorm(h_s, final_norm)
    fox_k_p, fox_v_p, fox_logf_p, mla_lat_p, mla_kr_p, gla_p, ssd_p, conv_p = [jnp.stack(t) for t in zip(*new_p)]
    fox_k_s, fox_v_s, fox_logf_s, mla_lat_s, mla_kr_s, gla_s, ssd_s, conv_s = [jnp.stack(t) for t in zip(*new_s)]
    return (y_prompt, y_sample,
            fox_k_p, fox_v_p, fox_logf_p, mla_lat_p, mla_kr_p, gla_p, ssd_p, conv_p,
            fox_k_s, fox_v_s, fox_logf_s, mla_lat_s, mla_kr_s, gla_s, ssd_s, conv_s)
```

```python
import functools
import math

import jax
import jax.numpy as jnp
import numpy as np
from jax import lax
from jax.experimental import pallas as pl
from jax.experimental.pallas import tpu as pltpu

F32 = jnp.float32
BF16 = jnp.bfloat16

D_MODEL = 4096
N_BRANCH = 4
BRANCH_W = D_MODEL // N_BRANCH
FOX_HEADS = 8
FOX_HEAD_DIM = 128
GLA_HEADS = 4
GLA_DK = 128
GLA_DV = 256
GLA_GATE_RANK = 16
GLA_TAU = 16.0
GLA_CHUNK = 64
SSD_HEAD_DIM = 64
SSD_HEADS = 16
SSD_STATE = 128
SSD_GROUPS = 4
SSD_CONV = 4
SSD_CHUNK = 128
SSD_CONV_CH = 2048
MLA_HEADS = 8
MLA_NOPE = 128
MLA_ROPE = 32
MLA_V = 128
MLA_Q_RANK = 768
MLA_KV_RANK = 256
MLA_SCALE = (MLA_NOPE + MLA_ROPE) ** -0.5
ROPE_THETA = 10000.0
N_EXPERTS = 64
TOP_K = 8
EXPERT_DIM = 256
SHARED_DIM = 1024
N_ROUTE_GROUPS = 8
TOPK_ROUTE_GROUPS = 4
ROUTED_SCALE = 2.5
PAGE_SIZE = 128
EPS = 1e-6

LANES = 128
ROW_GROUP = 8
VMEM_LIMIT = 52 * 1024 * 1024
HI = lax.Precision.HIGHEST

SEG = dict(f_q=(0, 1024), g_v=(1024, 1024), g_o=(2048, 1024), s_z=(3072, 1024), s_xbc=(4096, 2048),
           g_q=(6144, 512), g_k=(6656, 512), m_ckv=(7168, 256), f_k=(7424, 128), f_v=(7552, 128),
           m_cq=(7680, 768), small=(8448, 128), pad=(8576, 128), br_gate=(8704, 16384))
N_IN_R = 8704 + 16384
SM_KR, SM_FF, SM_GA, SM_DT = 0, 32, 40, 56


def _cblk(name, width):
    off = SEG[name][0]
    assert off % width == 0
    return off // width


def _cp(sem, vmem=VMEM_LIMIT):
    return pltpu.CompilerParams(dimension_semantics=sem, vmem_limit_bytes=vmem)


def _silu(x):
    return x * jax.nn.sigmoid(x)


def _adaln_kernel(c_ref, w_ref, b_ref, o_ref):
    a = _silu(c_ref[...]).astype(BF16)
    o_ref[...] = jnp.dot(a, w_ref[...].astype(BF16), preferred_element_type=F32) + b_ref[...]


def adaln(c, w, b, tn=512):
    m, k = c.shape
    n = w.shape[1]
    return pl.pallas_call(
        _adaln_kernel,
        out_shape=jax.ShapeDtypeStruct((m, n), F32),
        grid=(n // tn,),
        in_specs=[pl.BlockSpec((m, k), lambda j: (0, 0)),
                  pl.BlockSpec((k, tn), lambda j: (0, j)),
                  pl.BlockSpec((1, tn), lambda j: (0, j))],
        out_specs=pl.BlockSpec((m, tn), lambda j: (0, j)),
        compiler_params=_cp(("arbitrary",)),
        name="adaln",
    )(c, w, b.reshape(1, n))


def _norm_mod_kernel(x_ref, g_ref, sh_ref, sc_ref, *o_refs):
    x = x_ref[...]
    gb = x.shape[0]
    y = x * lax.rsqrt(jnp.mean(x * x, axis=-1, keepdims=True) + EPS) * g_ref[...]
    u = (y * (1.0 + sc_ref[0]) + sh_ref[0]).reshape(gb * ROW_GROUP, x.shape[-1])
    for o_ref in o_refs:
        o_ref[...] = u.astype(o_ref.dtype)


def _norm_kernel(x_ref, g_ref, o_ref):
    x = x_ref[...]
    o_ref[...] = (x * lax.rsqrt(jnp.mean(x * x, axis=-1, keepdims=True) + EPS) * g_ref[...]).astype(o_ref.dtype)


def norm_mod(x, g, mod6, i_shift, i_scale, out_dtypes=(BF16,), gb=32):
    m, d = x.shape
    ng = m // ROW_GROUP
    x3 = x.reshape(ng, ROW_GROUP, d)
    out_spec = pl.BlockSpec((gb * ROW_GROUP, d), lambda i: (i, 0))
    return pl.pallas_call(
        _norm_mod_kernel,
        out_shape=tuple(jax.ShapeDtypeStruct((m, d), dt) for dt in out_dtypes),
        grid=(ng // gb,),
        in_specs=[pl.BlockSpec((gb, ROW_GROUP, d), lambda i: (i, 0, 0)),
                  pl.BlockSpec((1, d), lambda i: (0, 0)),
                  pl.BlockSpec((1, gb, 1, d), lambda i: (i_shift, i, 0, 0)),
                  pl.BlockSpec((1, gb, 1, d), lambda i: (i_scale, i, 0, 0))],
        out_specs=tuple(out_spec for _ in out_dtypes),
        compiler_params=_cp(("arbitrary",)),
        name="norm_mod",
    )(x3, g.reshape(1, d), mod6, mod6)


def rmsnorm_rows(x, g, out_dtype, tm=256):
    m, d = x.shape
    return pl.pallas_call(
        _norm_kernel,
        out_shape=jax.ShapeDtypeStruct((m, d), out_dtype),
        grid=(m // tm,),
        in_specs=[pl.BlockSpec((tm, d), lambda i: (i, 0)), pl.BlockSpec((1, d), lambda i: (0, 0))],
        out_specs=pl.BlockSpec((tm, d), lambda i: (i, 0)),
        compiler_params=_cp(("arbitrary",)),
        name="rmsnorm_rows",
    )(x, g.reshape(1, d))


def _mm_kernel(a_ref, b_ref, o_ref):
    o_ref[...] = jnp.dot(a_ref[...], b_ref[...], preferred_element_type=F32).astype(o_ref.dtype)


def matmul(a, b, out_dtype=F32, tm=1024, tn=512):
    m, k = a.shape
    n = b.shape[1]
    tm, tn = min(tm, m), min(tn, n)
    return pl.pallas_call(
        _mm_kernel,
        out_shape=jax.ShapeDtypeStruct((m, n), out_dtype),
        grid=(m // tm, n // tn),
        in_specs=[pl.BlockSpec((tm, k), lambda i, j: (i, 0)), pl.BlockSpec((k, tn), lambda i, j: (0, j))],
        out_specs=pl.BlockSpec((tm, tn), lambda i, j: (i, j)),
        compiler_params=_cp(("arbitrary", "arbitrary")),
        name="matmul",
    )(a, b)


def _mm_resid_kernel(a_ref, b_ref, h_ref, gate_ref, *rest):
    if len(rest) == 2:
        extra_ref, o_ref = rest
    else:
        extra_ref, (o_ref,) = None, rest
    acc = jnp.dot(a_ref[...], b_ref[...], preferred_element_type=F32)
    if extra_ref is not None:
        acc = acc + extra_ref[...]
    tm, tn = acc.shape
    acc3 = acc.reshape(tm // ROW_GROUP, ROW_GROUP, tn) * gate_ref[0]
    o_ref[...] = h_ref[...] + acc3.reshape(tm, tn)


def matmul_resid(a, b, h, mod6, i_gate, extra=None, tm=512, tn=512):
    m, k = a.shape
    n = b.shape[1]
    gb = tm // ROW_GROUP
    in_specs = [pl.BlockSpec((tm, k), lambda i, j: (i, 0)),
                pl.BlockSpec((k, tn), lambda i, j: (0, j)),
                pl.BlockSpec((tm, tn), lambda i, j: (i, j)),
                pl.BlockSpec((1, gb, 1, tn), lambda i, j: (i_gate, i, 0, j))]
    args = [a, b, h, mod6]
    if extra is not None:
        in_specs.append(pl.BlockSpec((tm, tn), lambda i, j: (i, j)))
        args.append(extra)
    return pl.pallas_call(
        _mm_resid_kernel,
        out_shape=jax.ShapeDtypeStruct((m, n), F32),
        grid=(m // tm, n // tn),
        in_specs=in_specs,
        out_specs=pl.BlockSpec((tm, tn), lambda i, j: (i, j)),
        compiler_params=_cp(("arbitrary", "arbitrary")),
        name="matmul_resid",
    )(*args)


def _swiglu_kernel(a_ref, bg_ref, bu_ref, o_ref):
    a = a_ref[...]
    g = jnp.dot(a, bg_ref[...], preferred_element_type=F32)
    u = jnp.dot(a, bu_ref[...], preferred_element_type=F32)
    o_ref[...] = (_silu(g) * u).astype(o_ref.dtype)


def swiglu_up(a, bg, bu, tm=1024, tn=256):
    m, k = a.shape
    n = bg.shape[1]
    return pl.pallas_call(
        _swiglu_kernel,
        out_shape=jax.ShapeDtypeStruct((m, n), BF16),
        grid=(m // tm, n // tn),
        in_specs=[pl.BlockSpec((tm, k), lambda i, j: (i, 0)),
                  pl.BlockSpec((k, tn), lambda i, j: (0, j)),
                  pl.BlockSpec((k, tn), lambda i, j: (0, j))],
        out_specs=pl.BlockSpec((tm, tn), lambda i, j: (i, j)),
        compiler_params=_cp(("arbitrary", "arbitrary")),
        name="swiglu_up",
    )(a, bg, bu)


def _log_sigmoid(x):
    return jnp.minimum(x, 0.0) - jnp.log1p(jnp.exp(-jnp.abs(x)))


def _softplus(x):
    return jnp.maximum(x, 0.0) + jnp.log1p(jnp.exp(-jnp.abs(x)))


def _rope_lanes(x, c, s1, s2):
    return x * c + pltpu.roll(x, LANES - MLA_ROPE // 2, 1) * s1 + pltpu.roll(x, MLA_ROPE // 2, 1) * s2


def _prep_kernel(sm_ref, ckv_ref, cq_ref, brow_ref, rc_ref, rs1_ref, rs2_ref, wa_ref, ba_ref, gkv_ref, gq_ref,
                 sm_o, la_o, lat_o, latb_o, krp_o, cqn_o):
    x = sm_ref[...]
    xb = x + brow_ref[...]
    lane = lax.broadcasted_iota(jnp.int32, x.shape, 1)
    roped = _rope_lanes(x, rc_ref[...], rs1_ref[...], rs2_ref[...])
    out = jnp.where(lane < SM_FF, roped,
                    jnp.where(lane < SM_GA, _log_sigmoid(xb),
                              jnp.where((lane >= SM_DT) & (lane < SM_DT + SSD_HEADS), _softplus(xb), x)))
    sm_o[...] = out
    krp_o[...] = jnp.where(lane < MLA_ROPE, roped, 0.0).astype(BF16)
    ga = jnp.dot(x.astype(BF16), wa_ref[...], preferred_element_type=F32) + ba_ref[...]
    la_o[...] = _log_sigmoid(ga) * (1.0 / GLA_TAU)
    ckv = ckv_ref[...]
    lat = ckv * lax.rsqrt(jnp.mean(ckv * ckv, axis=-1, keepdims=True) + EPS) * gkv_ref[...]
    lat_o[...] = lat
    latb_o[...] = lat.astype(BF16)
    cq = cq_ref[...]
    cqn_o[...] = (cq * lax.rsqrt(jnp.mean(cq * cq, axis=-1, keepdims=True) + EPS) * gq_ref[...]).astype(BF16)


def prep_tokens(y, brow, rc, rs1, rs2, wa_pad, ba, gkv, gq, tm=512):
    m = y.shape[0]
    row = lambda w: pl.BlockSpec((1, w), lambda i: (0, 0))
    tok = lambda w: pl.BlockSpec((tm, w), lambda i: (i, 0))
    return pl.pallas_call(
        _prep_kernel,
        out_shape=(jax.ShapeDtypeStruct((m, LANES), F32), jax.ShapeDtypeStruct((m, 512), F32),
                   jax.ShapeDtypeStruct((m, MLA_KV_RANK), F32), jax.ShapeDtypeStruct((m, MLA_KV_RANK), BF16),
                   jax.ShapeDtypeStruct((m, LANES), BF16), jax.ShapeDtypeStruct((m, MLA_Q_RANK), BF16)),
        grid=(m // tm,),
        in_specs=[pl.BlockSpec((tm, LANES), lambda i: (i, _cblk("small", LANES))),
                  pl.BlockSpec((tm, MLA_KV_RANK), lambda i: (i, _cblk("m_ckv", MLA_KV_RANK))),
                  pl.BlockSpec((tm, MLA_Q_RANK), lambda i: (i, _cblk("m_cq", MLA_Q_RANK))),
                  row(LANES), tok(LANES), tok(LANES), tok(LANES),
                  pl.BlockSpec((LANES, 512), lambda i: (0, 0)), row(512), row(MLA_KV_RANK), row(MLA_Q_RANK)],
        out_specs=(tok(LANES), tok(512), tok(MLA_KV_RANK), tok(MLA_KV_RANK), tok(LANES), tok(MLA_Q_RANK)),
        compiler_params=_cp(("arbitrary",)),
        name="prep_tokens",
    )(y, y, y, brow, rc, rs1, rs2, wa_pad, ba.reshape(1, 512), gkv.reshape(1, -1), gq.reshape(1, -1))


def _mla_q_kernel(cqn_ref, wn_ref, wr_ref, wuk_ref, rc_ref, rs1_ref, rs2_ref, ql_o, qr_o):
    cq = cqn_ref[...]
    qn = jnp.dot(cq, wn_ref[...], preferred_element_type=F32)
    qr = jnp.dot(cq, wr_ref[...], preferred_element_type=F32)
    c, s1, s2 = rc_ref[...], rs1_ref[...], rs2_ref[...]
    for h in range(MLA_HEADS):
        sl = slice(h * LANES, (h + 1) * LANES)
        qr_o[h] = _rope_lanes(qr[:, sl], c, s1, s2).astype(BF16)
        ql_o[h] = jnp.dot(qn[:, sl].astype(BF16), wuk_ref[h], preferred_element_type=F32).astype(BF16)


def mla_q(cqn, wn, wr, wuk_t, rc, rs1, rs2, tm=512):
    m = cqn.shape[0]
    tok = lambda w: pl.BlockSpec((tm, w), lambda i: (i, 0))
    return pl.pallas_call(
        _mla_q_kernel,
        out_shape=(jax.ShapeDtypeStruct((MLA_HEADS, m, MLA_KV_RANK), BF16),
                   jax.ShapeDtypeStruct((MLA_HEADS, m, LANES), BF16)),
        grid=(m // tm,),
        in_specs=[tok(MLA_Q_RANK),
                  pl.BlockSpec((MLA_Q_RANK, MLA_HEADS * LANES), lambda i: (0, 0)),
                  pl.BlockSpec((MLA_Q_RANK, MLA_HEADS * LANES), lambda i: (0, 0)),
                  pl.BlockSpec((MLA_HEADS, MLA_NOPE, MLA_KV_RANK), lambda i: (0, 0, 0)),
                  tok(LANES), tok(LANES), tok(LANES)],
        out_specs=(pl.BlockSpec((MLA_HEADS, tm, MLA_KV_RANK), lambda i: (0, i, 0)),
                   pl.BlockSpec((MLA_HEADS, tm, LANES), lambda i: (0, i, 0))),
        compiler_params=_cp(("arbitrary",)),
        name="mla_q",
    )(cqn, wn, wr, wuk_t, rc, rs1, rs2)


NEG = -1e30
NT = (((1,), (1,)), ((), ()))
TN = (((0,), (0,)), ((), ()))


def _online_softmax_step(s, v, m_ref, l_ref, acc_ref):
    tk, dv = s.shape[1], acc_ref.shape[1]
    m_prev = m_ref[...]
    m_new = jnp.maximum(m_prev, jnp.max(s, axis=-1, keepdims=True))
    alpha = jnp.exp(m_prev - m_new)
    p = jnp.exp(s - pltpu.repeat(m_new, tk // LANES, axis=1))
    l_ref[...] = alpha * l_ref[...] + jnp.sum(p, axis=-1, keepdims=True)
    m_ref[...] = m_new
    pv = jnp.dot(p.astype(BF16), v, preferred_element_type=F32)
    acc_ref[...] = acc_ref[...] * pltpu.repeat(alpha, dv // LANES, axis=1) + pv


def _causal_mask(s3, qi, ki, tq, tk):
    qpos = qi * tq + lax.broadcasted_iota(jnp.int32, (1, tq, tk), 1)
    kpos = ki * tk + lax.broadcasted_iota(jnp.int32, (1, tq, tk), 2)
    return jnp.where(kpos <= qpos, s3, NEG)


def _init_softmax(m_ref, l_ref, acc_ref):
    m_ref[...] = jnp.full(m_ref.shape, NEG, F32)
    l_ref[...] = jnp.zeros(l_ref.shape, F32)
    acc_ref[...] = jnp.zeros(acc_ref.shape, F32)


def _fox_prompt_kernel(q_ref, k_ref, v_ref, ck_ref, o_ref, qs_ref, m_ref, l_ref, acc_ref, *, tq, tk):
    qi, ki, nk = pl.program_id(1), pl.program_id(2), pl.num_programs(2)
    hd = FOX_HEAD_DIM

    @pl.when(ki == 0)
    def _():
        _init_softmax(m_ref, l_ref, acc_ref)
        for h in range(FOX_HEADS):
            qs_ref[h * tq:(h + 1) * tq, :] = q_ref[:, h * hd:(h + 1) * hd].astype(BF16)

    @pl.when(ki * tk <= qi * tq + tq - 1)
    def _():
        s = lax.dot_general(qs_ref[...], k_ref[...].astype(BF16), NT, preferred_element_type=F32) * (hd ** -0.5)
        s3 = s.reshape(FOX_HEADS, tq, tk) - ck_ref[0][:, None, :]
        s3 = _causal_mask(s3, qi, ki, tq, tk)
        _online_softmax_step(s3.reshape(FOX_HEADS * tq, tk), v_ref[...].astype(BF16), m_ref, l_ref, acc_ref)

    @pl.when(ki == nk - 1)
    def _():
        o = acc_ref[...] / l_ref[...]
        for h in range(FOX_HEADS):
            o_ref[:, h * hd:(h + 1) * hd] = o[h * tq:(h + 1) * tq].astype(o_ref.dtype)


def fox_prompt(y, cum_t, batch, seq, tq=256, tk=512):
    nq, nk = seq // tq, seq // tk
    kmax = lambda qi, ki: jnp.minimum(ki, (qi * tq + tq - 1) // tk)
    w = FOX_HEADS * FOX_HEAD_DIM
    rows = FOX_HEADS * tq
    return pl.pallas_call(
        functools.partial(_fox_prompt_kernel, tq=tq, tk=tk),
        out_shape=jax.ShapeDtypeStruct((batch * seq, w), BF16),
        grid=(batch, nq, nk),
        in_specs=[pl.BlockSpec((tq, w), lambda b, qi, ki: (b * nq + qi, _cblk("f_q", w))),
                  pl.BlockSpec((tk, LANES), lambda b, qi, ki: (b * nk + kmax(qi, ki), _cblk("f_k", LANES))),
                  pl.BlockSpec((tk, LANES), lambda b, qi, ki: (b * nk + kmax(qi, ki), _cblk("f_v", LANES))),
                  pl.BlockSpec((1, FOX_HEADS, tk), lambda b, qi, ki: (b, 0, kmax(qi, ki)))],
        out_specs=pl.BlockSpec((tq, w), lambda b, qi, ki: (b * nq + qi, 0)),
        scratch_shapes=[pltpu.VMEM((rows, FOX_HEAD_DIM), BF16), pltpu.VMEM((rows, LANES), F32),
                        pltpu.VMEM((rows, LANES), F32), pltpu.VMEM((rows, FOX_HEAD_DIM), F32)],
        compiler_params=_cp(("arbitrary", "arbitrary", "arbitrary")),
        name="fox_prompt",
    )(y, y, y, cum_t)


def _mla_prompt_kernel(ql_ref, qr_ref, lat_ref, kr_ref, wuv_ref, o_ref, m_ref, l_ref, acc_ref, *, tq, tk):
    qi, ki, nk = pl.program_id(1), pl.program_id(2), pl.num_programs(2)
    nh = MLA_HEADS

    @pl.when(ki == 0)
    def _():
        _init_softmax(m_ref, l_ref, acc_ref)

    @pl.when(ki * tk <= qi * tq + tq - 1)
    def _():
        lat = lat_ref[...]
        s = lax.dot_general(ql_ref[...].reshape(nh * tq, MLA_KV_RANK), lat, NT, preferred_element_type=F32)
        s = s + lax.dot_general(qr_ref[...].reshape(nh * tq, LANES), kr_ref[...], NT, preferred_element_type=F32)
        s3 = _causal_mask((s * MLA_SCALE).reshape(nh, tq, tk), qi, ki, tq, tk)
        _online_softmax_step(s3.reshape(nh * tq, tk), lat, m_ref, l_ref, acc_ref)

    @pl.when(ki == nk - 1)
    def _():
        o = acc_ref[...] / pltpu.repeat(l_ref[...], MLA_KV_RANK // LANES, axis=1)
        for h in range(nh):
            oh = jnp.dot(o[h * tq:(h + 1) * tq].astype(BF16), wuv_ref[h], preferred_element_type=F32)
            o_ref[:, h * MLA_V:(h + 1) * MLA_V] = oh.astype(o_ref.dtype)


def mla_prompt(ql, qr, latb, krp, wuv_t, batch, seq, tq=256, tk=512):
    nq, nk = seq // tq, seq // tk
    kmax = lambda qi, ki: jnp.minimum(ki, (qi * tq + tq - 1) // tk)
    rows = MLA_HEADS * tq
    return pl.pallas_call(
        functools.partial(_mla_prompt_kernel, tq=tq, tk=tk),
        out_shape=jax.ShapeDtypeStruct((batch * seq, MLA_HEADS * MLA_V), BF16),
        grid=(batch, nq, nk),
        in_specs=[pl.BlockSpec((MLA_HEADS, tq, MLA_KV_RANK), lambda b, qi, ki: (0, b * nq + qi, 0)),
                  pl.BlockSpec((MLA_HEADS, tq, LANES), lambda b, qi, ki: (0, b * nq + qi, 0)),
                  pl.BlockSpec((tk, MLA_KV_RANK), lambda b, qi, ki: (b * nk + kmax(qi, ki), 0)),
                  pl.BlockSpec((tk, LANES), lambda b, qi, ki: (b * nk + kmax(qi, ki), 0)),
                  pl.BlockSpec((MLA_HEADS, MLA_KV_RANK, MLA_V), lambda b, qi, ki: (0, 0, 0))],
        out_specs=pl.BlockSpec((tq, MLA_HEADS * MLA_V), lambda b, qi, ki: (b * nq + qi, 0)),
        scratch_shapes=[pltpu.VMEM((rows, LANES), F32), pltpu.VMEM((rows, LANES), F32),
                        pltpu.VMEM((rows, MLA_KV_RANK), F32)],
        compiler_params=_cp(("arbitrary", "arbitrary", "arbitrary")),
        name="mla_prompt",
    )(ql, qr, latb, krp, wuv_t)


def _fox_sample_kernel(q_ref, k_ref, v_ref, suf_ref, kn_ref, vn_ref, bn_ref, o_ref, m_ref, l_ref, acc_ref):
    ki, nk = pl.program_id(1), pl.num_programs(1)
    rows, tk = q_ref.shape[1], k_ref.shape[1]
    nt = rows // FOX_HEADS
    scale = FOX_HEAD_DIM ** -0.5

    @pl.when(ki == 0)
    def _():
        _init_softmax(m_ref, l_ref, acc_ref)

    q = q_ref[0]
    s = lax.dot_general(q, k_ref[0].astype(BF16), NT, preferred_element_type=F32) * scale
    s3 = s.reshape(FOX_HEADS, nt, tk) + suf_ref[0][:, None, :]
    _online_softmax_step(s3.reshape(rows, tk), v_ref[0].astype(BF16), m_ref, l_ref, acc_ref)

    @pl.when(ki == nk - 1)
    def _():
        sn = lax.dot_general(q, kn_ref[0], NT, preferred_element_type=F32) * scale + bn_ref[0]
        _online_softmax_step(sn, vn_ref[0], m_ref, l_ref, acc_ref)
        o_ref[0] = acc_ref[...] / l_ref[...]


def fox_sample(q, kpast, vpast, suffix_t, knew, vnew, bnew, tk=2048):
    b, rows, hd = q.shape
    p = kpast.shape[1]
    per_seq = lambda r, c: pl.BlockSpec((1, r, c), lambda i, j: (i, 0, 0))
    return pl.pallas_call(
        _fox_sample_kernel,
        out_shape=jax.ShapeDtypeStruct((b, rows, hd), F32),
        grid=(b, p // tk),
        in_specs=[per_seq(rows, hd),
                  pl.BlockSpec((1, tk, hd), lambda i, j: (i, j, 0)),
                  pl.BlockSpec((1, tk, hd), lambda i, j: (i, j, 0)),
                  pl.BlockSpec((1, FOX_HEADS, tk), lambda i, j: (i, 0, j)),
                  per_seq(LANES, hd), per_seq(LANES, hd), per_seq(rows, LANES)],
        out_specs=per_seq(rows, hd),
        scratch_shapes=[pltpu.VMEM((rows, LANES), F32), pltpu.VMEM((rows, LANES), F32), pltpu.VMEM((rows, hd), F32)],
        compiler_params=_cp(("arbitrary", "arbitrary")),
        name="fox_sample",
    )(q, kpast, vpast, suffix_t, knew, vnew, bnew)


def _mla_sample_kernel(ql_ref, qr_ref, lat_ref, kr_ref, latn_ref, krn_ref, bn_ref, o_ref, m_ref, l_ref, acc_ref):
    ki, nk = pl.program_id(1), pl.num_programs(1)

    @pl.when(ki == 0)
    def _():
        _init_softmax(m_ref, l_ref, acc_ref)

    ql, qr = ql_ref[0], qr_ref[0][:, :MLA_ROPE]
    lat = lat_ref[0].astype(BF16)
    s = lax.dot_general(ql, lat, NT, preferred_element_type=F32)
    s = s + lax.dot_general(qr, kr_ref[0].astype(BF16), NT, preferred_element_type=F32)
    _online_softmax_step(s * MLA_SCALE, lat, m_ref, l_ref, acc_ref)

    @pl.when(ki == nk - 1)
    def _():
        latn = latn_ref[0]
        sn = lax.dot_general(ql, latn, NT, preferred_element_type=F32)
        sn = sn + lax.dot_general(qr, krn_ref[0], NT, preferred_element_type=F32)
        _online_softmax_step(sn * MLA_SCALE + bn_ref[0], latn, m_ref, l_ref, acc_ref)
        o_ref[0] = acc_ref[...] / pltpu.repeat(l_ref[...], MLA_KV_RANK // LANES, axis=1)


def mla_sample(ql, qr, latpast, krpast, latnew, krnew, bnew, tk=2048):
    b, rows, r = ql.shape
    p = latpast.shape[1]
    per_seq = lambda a, c: pl.BlockSpec((1, a, c), lambda i, j: (i, 0, 0))
    return pl.pallas_call(
        _mla_sample_kernel,
        out_shape=jax.ShapeDtypeStruct((b, rows, r), F32),
        grid=(b, p // tk),
        in_specs=[per_seq(rows, r), per_seq(rows, LANES),
                  pl.BlockSpec((1, tk, r), lambda i, j: (i, j, 0)),
                  pl.BlockSpec((1, tk, MLA_ROPE), lambda i, j: (i, j, 0)),
                  per_seq(LANES, r), per_seq(LANES, MLA_ROPE), per_seq(rows, LANES)],
        out_specs=per_seq(rows, r),
        scratch_shapes=[pltpu.VMEM((rows, LANES), F32), pltpu.VMEM((rows, LANES), F32), pltpu.VMEM((rows, r), F32)],
        compiler_params=_cp(("arbitrary", "arbitrary")),
        name="mla_sample",
    )(ql, qr, latpast, krpast, latnew, krnew, bnew)


def _lanes_to_rows(row_vec, n):
    return jnp.broadcast_to(row_vec, (n, n)).T


def _gla_kernel(q_ref, k_ref, v_ref, go_ref, la_ref, s0_ref, gn_ref, o_ref, sfin_ref, s_ref, *, chunk):
    c, nc = pl.program_id(1), pl.num_programs(1)
    dk, dv = GLA_DK, GLA_DV

    @pl.when(c == 0)
    def _():
        s_ref[...] = s0_ref[0]

    row = lax.broadcasted_iota(jnp.int32, (chunk, chunk), 0)
    col = lax.broadcasted_iota(jnp.int32, (chunk, chunk), 1)
    tril = col <= row
    gn = gn_ref[...]
    for h in range(GLA_HEADS):
        ksl, vsl = slice(h * dk, (h + 1) * dk), slice(h * dv, (h + 1) * dv)
        b = jnp.dot(tril.astype(F32), la_ref[:, ksl], precision=HI, preferred_element_type=F32)
        b_last = b[chunk - 1:chunk, :]
        q = q_ref[:, ksl] * (dk ** -0.5)
        k = k_ref[:, ksl]
        q_hat = (q * jnp.exp(b)).astype(BF16)
        k_hat = (k * jnp.exp(-b)).astype(BF16)
        k_tail = (k * jnp.exp(b_last - b)).astype(BF16)
        v = v_ref[:, vsl].astype(BF16)
        a = lax.dot_general(q_hat, k_hat, NT, preferred_element_type=F32)
        a = jnp.where(tril, a, 0.0).astype(BF16)
        s_old = s_ref[h]
        o = (jnp.dot(a, v, preferred_element_type=F32)
             + jnp.dot(q_hat, s_old.astype(BF16), preferred_element_type=F32))
        kv = lax.dot_general(k_tail, v, TN, preferred_element_type=F32)
        decay = _lanes_to_rows(jnp.exp(b_last), dk)
        s_ref[h] = s_old * pltpu.repeat(decay, dv // dk, axis=1) + kv
        on = o * lax.rsqrt(jnp.mean(o * o, axis=-1, keepdims=True) + EPS) * gn
        o_ref[:, vsl] = (on * _silu(go_ref[:, vsl])).astype(o_ref.dtype)

    @pl.when(c == nc - 1)
    def _():
        sfin_ref[0] = s_ref[...]


def gla(y, log_a, s0, gnorm, row_off, batch, seq, out_dtype):
    chunk = math.gcd(seq, GLA_CHUNK)
    nc = seq // chunk
    r0 = row_off // chunk
    qk_w, v_w = GLA_HEADS * GLA_DK, GLA_HEADS * GLA_DV
    rowblk = lambda w, cb: pl.BlockSpec((chunk, w), lambda b, c: (r0 + b * nc + c, cb))
    st = pl.BlockSpec((1, GLA_HEADS, GLA_DK, GLA_DV), lambda b, c: (b, 0, 0, 0))
    return pl.pallas_call(
        functools.partial(_gla_kernel, chunk=chunk),
        out_shape=(jax.ShapeDtypeStruct((batch * seq, v_w), out_dtype),
                   jax.ShapeDtypeStruct((batch, GLA_HEADS, GLA_DK, GLA_DV), F32)),
        grid=(batch, nc),
        in_specs=[rowblk(qk_w, _cblk("g_q", qk_w)), rowblk(qk_w, _cblk("g_k", qk_w)),
                  rowblk(v_w, _cblk("g_v", v_w)), rowblk(v_w, _cblk("g_o", v_w)),
                  rowblk(qk_w, 0), st, pl.BlockSpec((1, GLA_DV), lambda b, c: (0, 0))],
        out_specs=(pl.BlockSpec((chunk, v_w), lambda b, c: (b * nc + c, 0)), st),
        scratch_shapes=[pltpu.VMEM((GLA_HEADS, GLA_DK, GLA_DV), F32)],
        compiler_params=_cp(("arbitrary", "arbitrary")),
        name="gla",
    )(y, y, y, y, log_a, s0, gnorm.reshape(1, GLA_DV))


SSD_PAIRS = SSD_HEADS // 2
CONV_PAD = 8


def _ssd_kernel(xbc_ref, z_ref, dt_ref, dtt_ref, buf_ref, h0_ref, cw_ref, cb_ref, arow_ref, acol_ref,
                dx_ref, e_ref, nw_ref, o_ref, hfin_ref, h_ref, xpad_ref, *, chunk):
    c, nc = pl.program_id(1), pl.num_programs(1)
    n, hw = SSD_STATE, BRANCH_W

    @pl.when(c == 0)
    def _():
        h_ref[...] = h0_ref[0]
        xpad_ref[0:CONV_PAD, :] = buf_ref[0]

    xpad_ref[CONV_PAD:CONV_PAD + chunk, :] = xbc_ref[...]
    conv = cb_ref[...]
    for kk in range(SSD_CONV):
        lo = CONV_PAD - (SSD_CONV - 1) + kk
        conv = conv + xpad_ref[lo:lo + chunk, :] * cw_ref[kk:kk + 1, :]
    xpad_ref[0:CONV_PAD, :] = xpad_ref[chunk:chunk + CONV_PAD, :]
    xbc = _silu(conv)
    x, bm, cm = xbc[:, :hw], xbc[:, hw:hw + SSD_GROUPS * n], xbc[:, hw + SSD_GROUPS * n:]

    row = lax.broadcasted_iota(jnp.int32, (chunk, chunk), 0)
    col = lax.broadcasted_iota(jnp.int32, (chunk, chunk), 1)
    tril = col <= row
    dt, dtt = dt_ref[...], dtt_ref[0]
    cum = jnp.dot(tril.astype(F32), dt * arow_ref[...], precision=HI, preferred_element_type=F32)
    cum_t = jnp.dot(dtt * acol_ref[...], (row <= col).astype(F32), precision=HI, preferred_element_type=F32)
    expand = e_ref[...]
    cum_x = jnp.dot(cum, expand, precision=HI, preferred_element_type=F32)
    dt_x = jnp.dot(dt, expand, precision=HI, preferred_element_type=F32)
    cum_last_x = cum_x[chunk - 1:chunk, :]
    q_decay_x = jnp.exp(cum_x)
    w_s_x = jnp.exp(cum_last_x - cum_x) * dt_x
    chunk_decay_x = jnp.exp(cum_last_x)
    lane = lax.broadcasted_iota(jnp.int32, (chunk, LANES), 1)
    first_half = lane < SSD_HEAD_DIM
    zs = _silu(z_ref[...])
    gw = hw // SSD_GROUPS

    for g in range(SSD_GROUPS):
        bg = bm[:, g * n:(g + 1) * n].astype(BF16)
        cg = cm[:, g * n:(g + 1) * n].astype(BF16)
        cb = lax.dot_general(cg, bg, NT, preferred_element_type=F32)
        ys = []
        for pp in range(gw // LANES):
            pair = g * (gw // LANES) + pp
            sl = slice(pair * LANES, (pair + 1) * LANES)
            xb = x[:, sl]
            y = dx_ref[:, sl] * xb
            for half in range(2):
                hd = 2 * pair + half
                seg = cum[:, hd:hd + 1] - cum_t[hd:hd + 1, :]
                mm = cb * jnp.exp(jnp.where(tril, seg, NEG)) * dtt[hd:hd + 1, :]
                xh = jnp.where(first_half, xb, 0.0) if half == 0 else jnp.where(first_half, 0.0, xb)
                y = y + jnp.dot(mm.astype(BF16), xh.astype(BF16), preferred_element_type=F32)
            hp = h_ref[pair]
            y = y + lax.dot_general(cg, hp.astype(BF16), NT, preferred_element_type=F32) * q_decay_x[:, sl]
            dstate = lax.dot_general((xb * w_s_x[:, sl]).astype(BF16), bg, TN, preferred_element_type=F32)
            h_ref[pair] = hp * _lanes_to_rows(chunk_decay_x[:, sl], LANES) + dstate
            ys.append(y * zs[:, sl])
        ms = sum(jnp.sum(yy * yy, axis=-1, keepdims=True) for yy in ys) * (1.0 / gw)
        inv = lax.rsqrt(ms + EPS)
        for pp, yy in enumerate(ys):
            sl = slice(g * gw + pp * LANES, g * gw + (pp + 1) * LANES)
            o_ref[:, sl] = (yy * inv * nw_ref[:, sl]).astype(o_ref.dtype)

    @pl.when(c == nc - 1)
    def _():
        hfin_ref[0] = h_ref[...]


def ssd(y, dt, buf, h0, conv_w, conv_b, a_neg, d_skip, norm_w, row_off, batch, seq, out_dtype):
    chunk = math.gcd(seq, SSD_CHUNK)
    nc = seq // chunk
    r0 = row_off // chunk
    hw, ch = BRANCH_W, SSD_CONV_CH
    dt_g = dt[row_off:row_off + batch * seq]
    dtt = dt_g.reshape(batch * nc, chunk, SSD_HEADS).transpose(0, 2, 1)
    bufp = jnp.pad(buf, ((0, 0), (CONV_PAD - (SSD_CONV - 1), 0), (0, 0)))
    hp0 = h0.reshape(batch, SSD_PAIRS, LANES, SSD_STATE)
    expand = jnp.repeat(jnp.eye(SSD_HEADS, dtype=F32), SSD_HEAD_DIM, axis=1)
    dx = jnp.repeat(d_skip, SSD_HEAD_DIM).reshape(1, hw)
    rowblk = lambda w, cb, off: pl.BlockSpec((chunk, w), lambda b, c: (off + b * nc + c, cb))
    const = lambda shp: pl.BlockSpec(shp, lambda b, c: (0,) * len(shp))
    st = pl.BlockSpec((1, SSD_PAIRS, LANES, SSD_STATE), lambda b, c: (b, 0, 0, 0))
    out, hfin = pl.pallas_call(
        functools.partial(_ssd_kernel, chunk=chunk),
        out_shape=(jax.ShapeDtypeStruct((batch * seq, hw), out_dtype),
                   jax.ShapeDtypeStruct((batch, SSD_PAIRS, LANES, SSD_STATE), F32)),
        grid=(batch, nc),
        in_specs=[rowblk(ch, _cblk("s_xbc", ch), r0), rowblk(hw, _cblk("s_z", hw), r0),
                  rowblk(SSD_HEADS, 0, 0),
                  pl.BlockSpec((1, SSD_HEADS, chunk), lambda b, c: (b * nc + c, 0, 0)),
                  pl.BlockSpec((1, CONV_PAD, ch), lambda b, c: (b, 0, 0)), st,
                  const((SSD_CONV, ch)), const((1, ch)), const((1, SSD_HEADS)), const((SSD_HEADS, 1)),
                  const((1, hw)), const((SSD_HEADS, hw)), const((1, hw))],
        out_specs=(pl.BlockSpec((chunk, hw), lambda b, c: (b * nc + c, 0)), st),
        scratch_shapes=[pltpu.VMEM((SSD_PAIRS, LANES, SSD_STATE), F32), pltpu.VMEM((chunk + CONV_PAD, ch), F32)],
        compiler_params=_cp(("arbitrary", "arbitrary")),
        name="ssd",
    )(y, y, dt_g, dtt, bufp, hp0, conv_w, conv_b.reshape(1, ch), a_neg.reshape(1, SSD_HEADS),
      a_neg.reshape(SSD_HEADS, 1), dx, expand, norm_w.reshape(1, hw))
    return out, hfin.reshape(batch, SSD_HEADS, SSD_HEAD_DIM, SSD_STATE)


def _bmm_kernel(a_ref, b_ref, o_ref):
    o_ref[0] = jnp.dot(a_ref[0], b_ref[0], preferred_element_type=F32).astype(o_ref.dtype)


def bmm(a, b, out_dtype):
    nb, m, k = a.shape
    n = b.shape[2]
    return pl.pallas_call(
        _bmm_kernel,
        out_shape=jax.ShapeDtypeStruct((nb, m, n), out_dtype),
        grid=(nb,),
        in_specs=[pl.BlockSpec((1, m, k), lambda i: (i, 0, 0)), pl.BlockSpec((1, k, n), lambda i: (i, 0, 0))],
        out_specs=pl.BlockSpec((1, m, n), lambda i: (i, 0, 0)),
        compiler_params=_cp(("arbitrary",)),
        name="bmm",
    )(a, b)


def _merge_kernel(o0, o1, o2, o3, wb_ref, g0, g1, g2, g3, out_ref):
    acc = None
    for n, (o_ref, g_ref) in enumerate(((o0, g0), (o1, g1), (o2, g2), (o3, g3))):
        t = jax.nn.sigmoid(g_ref[...]) * jnp.dot(o_ref[...], wb_ref[n], preferred_element_type=F32)
        acc = t if acc is None else acc + t
    out_ref[...] = acc.astype(out_ref.dtype)


def merge_branches(branches, wb, y, tm=1024, tn=512):
    m, w = branches[0].shape
    d = wb.shape[2]
    g0 = SEG["br_gate"][0]
    assert g0 % tn == 0 and d % tn == 0
    o_spec = pl.BlockSpec((tm, w), lambda i, j: (i, 0))
    gate = lambda n: pl.BlockSpec((tm, tn), lambda i, j: (i, (g0 + n * d) // tn + j))
    return pl.pallas_call(
        _merge_kernel,
        out_shape=jax.ShapeDtypeStruct((m, d), BF16),
        grid=(m // tm, d // tn),
        in_specs=[o_spec] * N_BRANCH + [pl.BlockSpec((N_BRANCH, w, tn), lambda i, j: (0, 0, j))]
                 + [gate(n) for n in range(N_BRANCH)],
        out_specs=pl.BlockSpec((tm, tn), lambda i, j: (i, j)),
        compiler_params=_cp(("arbitrary", "arbitrary")),
        name="merge_branches",
    )(*branches, wb, y, y, y, y)


def _router_kernel(x_ref, w_ref, b_ref, idx_o, wt_o):
    logits = jnp.dot(x_ref[...], w_ref[...], precision=HI, preferred_element_type=F32)
    s = jax.nn.sigmoid(logits)
    sb = s + b_ref[...]
    lane = lax.broadcasted_iota(jnp.int32, s.shape, 1).astype(F32)
    per_group = N_EXPERTS // N_ROUTE_GROUPS
    big = float(LANES)

    def first_argmax(v):
        m = jnp.max(v, axis=-1, keepdims=True)
        return m, jnp.min(jnp.where(v == m, lane, big), axis=-1, keepdims=True)

    in_group, gscore = [], []
    for g in range(N_ROUTE_GROUPS):
        ing = (lane >= g * per_group) & (lane < (g + 1) * per_group)
        v = jnp.where(ing, sb, NEG)
        m1, i1 = first_argmax(v)
        m2 = jnp.max(jnp.where(lane == i1, NEG, v), axis=-1, keepdims=True)
        in_group.append(ing)
        gscore.append(m1 + m2)
    allowed = jnp.zeros(s.shape, jnp.bool_)
    for g in range(N_ROUTE_GROUPS):
        ahead = jnp.zeros(gscore[g].shape, jnp.int32)
        for g2 in range(N_ROUTE_GROUPS):
            if g2 == g:
                continue
            beats = (gscore[g2] >= gscore[g]) if g2 < g else (gscore[g2] > gscore[g])
            ahead = ahead + beats.astype(jnp.int32)
        allowed = allowed | (in_group[g] & (ahead < TOPK_ROUTE_GROUPS))
    v = jnp.where(allowed, sb, NEG)
    idx_out = jnp.zeros(s.shape, jnp.int32)
    wt_out = jnp.zeros(s.shape, F32)
    wsum = jnp.zeros((s.shape[0], 1), F32)
    for kk in range(TOP_K):
        _, ik = first_argmax(v)
        hit = lane == ik
        wk = jnp.sum(jnp.where(hit, s, 0.0), axis=-1, keepdims=True)
        v = jnp.where(hit, NEG, v)
        idx_out = jnp.where(lane == kk, ik.astype(jnp.int32), idx_out)
        wt_out = jnp.where(lane == kk, wk, wt_out)
        wsum = wsum + wk
    idx_o[...] = idx_out
    wt_o[...] = wt_out / wsum * ROUTED_SCALE


def router(x, w_pad, b_pad, tm=256):
    m, d = x.shape
    return pl.pallas_call(
        _router_kernel,
        out_shape=(jax.ShapeDtypeStruct((m, LANES), jnp.int32), jax.ShapeDtypeStruct((m, LANES), F32)),
        grid=(m // tm,),
        in_specs=[pl.BlockSpec((tm, d), lambda i: (i, 0)), pl.BlockSpec((d, LANES), lambda i: (0, 0)),
                  pl.BlockSpec((1, LANES), lambda i: (0, 0))],
        out_specs=(pl.BlockSpec((tm, LANES), lambda i: (i, 0)), pl.BlockSpec((tm, LANES), lambda i: (i, 0))),
        compiler_params=_cp(("arbitrary",)),
        name="router",
    )(x, w_pad, b_pad)


def _experts_kernel(te_ref, tv_ref, x_ref, rw_ref, wg_ref, wu_ref, wd_ref, o_ref):
    i = pl.program_id(0)

    @pl.when(tv_ref[i] > 0)
    def _():
        x = x_ref[...]
        g = jnp.dot(x, wg_ref[0].astype(BF16), preferred_element_type=F32)
        u = jnp.dot(x, wu_ref[0].astype(BF16), preferred_element_type=F32)
        hid = (_silu(g) * u * rw_ref[...]).astype(BF16)
        o_ref[...] = jnp.dot(hid, wd_ref[0].astype(BF16), preferred_element_type=F32).astype(o_ref.dtype)

    @pl.when(tv_ref[i] == 0)
    def _():
        o_ref[...] = jnp.zeros(o_ref.shape, o_ref.dtype)


def routed_experts(xs, row_w, tile_expert, tile_valid, wg, wu, wd, te):
    r, d = xs.shape
    f = wg.shape[2]
    grid_spec = pltpu.PrefetchScalarGridSpec(
        num_scalar_prefetch=2,
        grid=(r // te,),
        in_specs=[pl.BlockSpec((te, d), lambda i, e, v: (i, 0)),
                  pl.BlockSpec((te, 1), lambda i, e, v: (i, 0)),
                  pl.BlockSpec((1, d, f), lambda i, e, v: (e[i], 0, 0)),
                  pl.BlockSpec((1, d, f), lambda i, e, v: (e[i], 0, 0)),
                  pl.BlockSpec((1, f, d), lambda i, e, v: (e[i], 0, 0))],
        out_specs=pl.BlockSpec((te, d), lambda i, e, v: (i, 0)),
    )
    return pl.pallas_call(
        _experts_kernel,
        out_shape=jax.ShapeDtypeStruct((r, d), F32),
        grid_spec=grid_spec,
        compiler_params=_cp(("arbitrary",)),
        name="routed_experts",
    )(tile_expert, tile_valid, xs, row_w, wg, wu, wd)


EXPERT_TILE = 256


def moe_dispatch(eidx, wts):
    t, k = eidx.shape
    te = EXPERT_TILE
    n_rows = t * k + N_EXPERTS * te
    n_tiles = n_rows // te
    flat_e = eidx.reshape(-1)
    order = jnp.argsort(flat_e, stable=True)
    sorted_e = flat_e[order]
    counts = jnp.zeros((N_EXPERTS,), jnp.int32).at[flat_e].add(1)
    padded = (counts + te - 1) // te * te
    pad_end = jnp.cumsum(padded)
    seg_start = jnp.cumsum(counts) - counts
    dest_sorted = (pad_end - padded)[sorted_e] + jnp.arange(t * k, dtype=jnp.int32) - seg_start[sorted_e]
    src_tok = jnp.zeros((n_rows,), jnp.int32).at[dest_sorted].set((order // k).astype(jnp.int32))
    row_w = jnp.zeros((n_rows,), F32).at[dest_sorted].set(wts.reshape(-1)[order])
    dest_flat = jnp.zeros((t * k,), jnp.int32).at[order].set(dest_sorted)
    tile_start = jnp.arange(n_tiles, dtype=jnp.int32) * te
    tile_expert = jnp.minimum(jnp.searchsorted(pad_end, tile_start, side="right"), N_EXPERTS - 1).astype(jnp.int32)
    tile_valid = (tile_start < pad_end[-1]).astype(jnp.int32)
    return src_tok, row_w.reshape(n_rows, 1), dest_flat, tile_expert, tile_valid


_ORIG_SPLITS = (("f_q", 1024), ("f_k", 128), ("f_v", 128), ("f_f", 8), ("g_q", 512), ("g_k", 512), ("g_v", 1024),
                ("g_o", 1024), ("g_a", 16), ("s_z", 1024), ("s_xbc", 2048), ("s_dt", 16), ("m_cq", 768),
                ("m_ckv", 256), ("m_kr", 32), ("br_gate", 16384))


def _relayout_w_in(w):
    src, off = {}, 0
    for name, width in _ORIG_SPLITS:
        src[name] = w[:, off:off + width]
        off += width
    k = w.shape[0]
    small = jnp.concatenate([src["m_kr"], src["f_f"], src["g_a"], src["s_dt"],
                             jnp.zeros((k, LANES - 72), w.dtype)], axis=1)
    parts = dict(src, small=small, pad=jnp.zeros((k, SEG["pad"][1]), w.dtype))
    order = sorted(SEG, key=lambda n: SEG[n][0])
    return jnp.concatenate([parts[n] for n in order], axis=1).astype(BF16)


def _seg(y, name, lo=None, hi=None):
    off, width = SEG[name]
    return y[lo:hi, off:off + width]


def _rope_tables(pos):
    half = MLA_ROPE // 2
    freqs = ROPE_THETA ** (-jnp.arange(half, dtype=F32) / half)
    ang = pos.astype(F32)[:, None] * freqs[None, :]
    cos, sin = jnp.cos(ang), jnp.sin(ang)
    z = jnp.zeros((pos.shape[0], LANES - MLA_ROPE), F32)
    zh = jnp.zeros((pos.shape[0], half), F32)
    return (jnp.concatenate([cos, cos, z], axis=1), jnp.concatenate([-sin, zh, z], axis=1),
            jnp.concatenate([zh, sin, z], axis=1))


def _new_key_bias(key_bias, n_tok):
    b, nh, _ = key_bias.shape
    t = jnp.arange(n_tok)
    causal = t[None, :] <= t[:, None]
    bias = jnp.where(causal[None, None], key_bias[:, :, None, :], NEG).reshape(b, nh * n_tok, n_tok)
    return jnp.pad(bias, ((0, 0), (0, 0), (0, LANES - n_tok)), constant_values=NEG)


def _pad_keys(x):
    return jnp.pad(x, ((0, 0), (0, LANES - x.shape[1]), (0, 0))).astype(BF16)


def _head_major(x, nb, nt, nh):
    w = x.shape[1] // nh
    return x.reshape(nb, nt, nh, w).transpose(0, 2, 1, 3).reshape(nb, nh * nt, w)


def _token_major(x, nb, nt, nh):
    w = x.shape[2]
    return x.reshape(nb, nh, nt, w).transpose(0, 2, 1, 3).reshape(nb * nt, nh * w)


def _layer(l, h, cpad, gidx, ropes, dims, caches, page_table, p):
    nb, ns, bs, nl = dims
    mp, ms = nb * ns, bs * nl
    d = D_MODEL
    rc, rs1, rs2 = ropes
    mod = adaln(cpad, p["w_ada"][l], p["b_ada"][l]).reshape(cpad.shape[0], 6, d)
    mod6 = mod[gidx].transpose(1, 0, 2)[:, :, None, :]

    (u,) = norm_mod(h, p["norm1"][l], mod6, 0, 1)
    y = matmul(u, _relayout_w_in(p["w_in"][l]))

    brow = (jnp.zeros((1, LANES), F32).at[0, SM_FF:SM_FF + FOX_HEADS].set(p["fox_fb"][l])
            .at[0, SM_DT:SM_DT + SSD_HEADS].set(p["ssd_dt_bias"][l]))
    wa_pad = jnp.zeros((LANES, GLA_HEADS * GLA_DK), F32).at[SM_GA:SM_GA + GLA_GATE_RANK].set(p["gla_w_a2"][l])
    small, log_a, lat, latb, krp, cqn = prep_tokens(y, brow, rc, rs1, rs2, wa_pad.astype(BF16), p["gla_b_a"][l],
                                                    p["mla_kv_norm"][l], p["mla_q_norm"][l])
    logf = small[:, SM_FF:SM_FF + FOX_HEADS]
    dt = small[:, SM_DT:SM_DT + SSD_HEADS]
    kr = small[:, SM_KR:SM_KR + MLA_ROPE]

    cum_p = jnp.cumsum(logf[:mp].reshape(nb, ns, FOX_HEADS), axis=1)
    o_fox_p = fox_prompt(y, cum_p.transpose(0, 2, 1), nb, ns)
    past_len = page_table.shape[1] * PAGE_SIZE
    gather = lambda pool: pool[l][page_table].reshape((bs, past_len) + pool.shape[3:])
    logf_past = gather(caches["fox_logf"])
    suffix = lax.cumsum(logf_past, axis=1, reverse=True) - logf_past
    cum_s = jnp.cumsum(logf[mp:].reshape(bs, nl, FOX_HEADS), axis=1)
    o_fox_s = fox_sample(_head_major(_seg(y, "f_q", mp), bs, nl, FOX_HEADS).astype(BF16),
                         gather(caches["fox_k"])[:, :, 0], gather(caches["fox_v"])[:, :, 0],
                         suffix.transpose(0, 2, 1),
                         _pad_keys(_seg(y, "f_k", mp).reshape(bs, nl, FOX_HEAD_DIM)),
                         _pad_keys(_seg(y, "f_v", mp).reshape(bs, nl, FOX_HEAD_DIM)),
                         _new_key_bias(-cum_s.transpose(0, 2, 1), nl))
    o_fox = jnp.concatenate([o_fox_p, _token_major(o_fox_s, bs, nl, FOX_HEADS).astype(BF16)], axis=0)

    o_gla_p, s_p = gla(y, log_a, jnp.zeros((nb, GLA_HEADS, GLA_DK, GLA_DV), F32), p["gla_norm"][l], 0, nb, ns, BF16)
    o_gla_s, s_s = gla(y, log_a, caches["gla"][l], p["gla_norm"][l], mp, bs, nl, F32)
    o_gla = jnp.concatenate([o_gla_p, o_gla_s.astype(BF16)], axis=0)

    a_neg = -jnp.exp(p["ssd_A_log"][l])
    ssd_w = (p["ssd_conv_w"][l], p["ssd_conv_b"][l], a_neg, p["ssd_D"][l], p["ssd_norm"][l])
    o_ssd_p, hs_p = ssd(y, dt, jnp.zeros((nb, SSD_CONV - 1, SSD_CONV_CH), F32),
                        jnp.zeros((nb, SSD_HEADS, SSD_HEAD_DIM, SSD_STATE), F32), *ssd_w, 0, nb, ns, BF16)
    o_ssd_s, hs_s = ssd(y, dt, caches["conv"][l], caches["ssd"][l], *ssd_w, mp, bs, nl, F32)
    o_ssd = jnp.concatenate([o_ssd_p, o_ssd_s.astype(BF16)], axis=0)
    xbc_p = _seg(y, "s_xbc", 0, mp).reshape(nb, ns, SSD_CONV_CH)
    xbc_s = jnp.concatenate([caches["conv"][l], _seg(y, "s_xbc", mp).reshape(bs, nl, SSD_CONV_CH)], axis=1)
    conv_p, conv_s = xbc_p[:, ns - (SSD_CONV - 1):], xbc_s[:, nl:]

    w_uq = p["mla_w_uq"][l]
    wn = w_uq[:, :, :MLA_NOPE].reshape(MLA_Q_RANK, MLA_HEADS * MLA_NOPE).astype(BF16)
    wr = jnp.pad(w_uq[:, :, MLA_NOPE:], ((0, 0), (0, 0), (0, LANES - MLA_ROPE)))
    wr = wr.reshape(MLA_Q_RANK, MLA_HEADS * LANES).astype(BF16)
    wuk_t = p["mla_w_uk"][l].transpose(1, 2, 0).astype(BF16)
    wuv_t = p["mla_w_uv"][l].transpose(1, 0, 2).astype(BF16)
    ql, qr = mla_q(cqn, wn, wr, wuk_t, rc, rs1, rs2)
    o_mla_p = mla_prompt(ql, qr, latb, krp, wuv_t, nb, ns)
    hm = lambda x: x[:, mp:].reshape(MLA_HEADS, bs, nl, x.shape[2]).transpose(1, 0, 2, 3).reshape(bs, MLA_HEADS * nl, -1)
    o_lat_s = mla_sample(hm(ql), hm(qr), gather(caches["mla_lat"]), gather(caches["mla_kr"]),
                         _pad_keys(latb[mp:].reshape(bs, nl, MLA_KV_RANK)),
                         _pad_keys(krp[mp:, :MLA_ROPE].reshape(bs, nl, MLA_ROPE)),
                         _new_key_bias(jnp.zeros((bs, MLA_HEADS, nl), F32), nl))
    o_lat_h = o_lat_s.reshape(bs, MLA_HEADS, nl, MLA_KV_RANK).transpose(1, 0, 2, 3).reshape(MLA_HEADS, ms, MLA_KV_RANK)
    o_mla_s = bmm(o_lat_h.astype(BF16), wuv_t, BF16).transpose(1, 0, 2).reshape(ms, MLA_HEADS * MLA_V)
    o_mla = jnp.concatenate([o_mla_p, o_mla_s], axis=0)

    merged = merge_branches((o_fox, o_gla, o_ssd, o_mla), p["w_branch"][l].astype(BF16), y)
    h = matmul_resid(merged, p["w_out"][l].astype(BF16), h, mod6, 2)

    u2, u2f = norm_mod(h, p["norm2"][l], mod6, 3, 4, out_dtypes=(BF16, F32))
    w_router = jnp.pad(p["w_router"][l], ((0, 0), (0, LANES - N_EXPERTS)))
    b_router = jnp.pad(p["router_bias"][l], (0, LANES - N_EXPERTS)).reshape(1, LANES)
    eidx, wts = router(u2f, w_router, b_router)
    src_tok, row_w, dest_flat, tile_expert, tile_valid = moe_dispatch(eidx[:, :TOP_K], wts[:, :TOP_K])
    ys = routed_experts(u2[src_tok], row_w, tile_expert, tile_valid,
                        p["w_exp_gate"][l], p["w_exp_up"][l], p["w_exp_down"][l], EXPERT_TILE)
    routed = ys[dest_flat].reshape(mp + ms, TOP_K, d).sum(axis=1)
    hid = swiglu_up(u2, p["w_sh_gate"][l].astype(BF16), p["w_sh_up"][l].astype(BF16))
    h = matmul_resid(hid, p["w_sh_down"][l].astype(BF16), h, mod6, 5, extra=routed)

    def split(x, tail_p, tail_s):
        return x[:mp].reshape((nb, ns) + tail_p), x[mp:].reshape((bs, nl) + tail_s)

    fk = split(_seg(y, "f_k"), (1, FOX_HEAD_DIM), (1, FOX_HEAD_DIM))
    fv = split(_seg(y, "f_v"), (1, FOX_HEAD_DIM), (1, FOX_HEAD_DIM))
    lf = split(logf, (FOX_HEADS,), (FOX_HEADS,))
    la = split(lat, (MLA_KV_RANK,), (MLA_KV_RANK,))
    kk = split(kr, (MLA_ROPE,), (MLA_ROPE,))
    new_p = (fk[0], fv[0], lf[0], la[0], kk[0], s_p, hs_p, conv_p)
    new_s = (fk[1], fv[1], lf[1], la[1], kk[1], s_s, hs_s, conv_s)
    return h, new_p, new_s


def kernel(x_prompt, x_sample, cache_fox_k, cache_fox_v, cache_fox_logf, cache_mla_latent, cache_mla_krope, state_gla, state_ssd, state_conv, page_table, c_prompt, c_sample, norm1, w_ada, b_ada, w_in, fox_fb, gla_w_a2, gla_b_a, gla_norm, ssd_conv_w, ssd_conv_b, ssd_dt_bias, ssd_A_log, ssd_D, ssd_norm, mla_q_norm, mla_w_uq, mla_kv_norm, mla_w_uk, mla_w_uv, w_branch, w_out, norm2, w_router, router_bias, w_exp_gate, w_exp_up, w_exp_down, w_sh_gate, w_sh_up, w_sh_down, final_norm):
    nb, ns, d = x_prompt.shape
    bs, nl, _ = x_sample.shape
    mp, ms = nb * ns, bs * nl
    depth = w_in.shape[0]
    past_len = page_table.shape[1] * PAGE_SIZE
    p = dict(norm1=norm1, w_ada=w_ada, b_ada=b_ada, w_in=w_in, fox_fb=fox_fb, gla_w_a2=gla_w_a2, gla_b_a=gla_b_a,
             gla_norm=gla_norm, ssd_conv_w=ssd_conv_w, ssd_conv_b=ssd_conv_b, ssd_dt_bias=ssd_dt_bias,
             ssd_A_log=ssd_A_log, ssd_D=ssd_D, ssd_norm=ssd_norm, mla_q_norm=mla_q_norm, mla_w_uq=mla_w_uq,
             mla_kv_norm=mla_kv_norm, mla_w_uk=mla_w_uk, mla_w_uv=mla_w_uv, w_branch=w_branch, w_out=w_out,
             norm2=norm2, w_router=w_router, router_bias=router_bias, w_exp_gate=w_exp_gate, w_exp_up=w_exp_up,
             w_exp_down=w_exp_down, w_sh_gate=w_sh_gate, w_sh_up=w_sh_up, w_sh_down=w_sh_down)
    caches = dict(fox_k=cache_fox_k, fox_v=cache_fox_v, fox_logf=cache_fox_logf, mla_lat=cache_mla_latent,
                  mla_kr=cache_mla_krope, gla=state_gla, ssd=state_ssd, conv=state_conv)

    h = jnp.concatenate([x_prompt.reshape(mp, d), x_sample.reshape(ms, d)], axis=0)
    n_cond = nb + bs
    cpad = jnp.pad(jnp.concatenate([c_prompt, c_sample], axis=0), ((0, -n_cond % 16), (0, 0)))
    gidx = jnp.concatenate([jnp.repeat(jnp.arange(nb), ns // ROW_GROUP),
                            nb + jnp.repeat(jnp.arange(bs), nl // ROW_GROUP)])
    pos = jnp.concatenate([jnp.tile(jnp.arange(ns), nb), jnp.tile(past_len + jnp.arange(nl), bs)])
    ropes = _rope_tables(pos)

    new_p, new_s = [], []
    for l in range(depth):
        h, st_p, st_s = _layer(l, h, cpad, gidx, ropes, (nb, ns, bs, nl), caches, page_table, p)
        new_p.append(st_p)
        new_s.append(st_s)
    yn = rmsnorm_rows(h, final_norm, F32)
    outs_p = [jnp.stack(t) for t in zip(*new_p)]
    outs_s = [jnp.stack(t) for t in zip(*new_s)]
    return (yn[:mp].reshape(nb, ns, d), yn[mp:].reshape(bs, nl, d), *outs_p, *outs_s)
```

```python
import functools
import math

import jax
import jax.numpy as jnp
import numpy as np
from jax import lax
from jax.experimental import pallas as pl
from jax.experimental.pallas import tpu as pltpu

F32 = jnp.float32
BF16 = jnp.bfloat16

D_MODEL = 4096
N_BRANCH = 4
BRANCH_W = D_MODEL // N_BRANCH
FOX_HEADS = 8
FOX_HEAD_DIM = 128
GLA_HEADS = 4
GLA_DK = 128
GLA_DV = 256
GLA_GATE_RANK = 16
GLA_TAU = 16.0
GLA_CHUNK = 64
SSD_HEAD_DIM = 64
SSD_HEADS = 16
SSD_STATE = 128
SSD_GROUPS = 4
SSD_CONV = 4
SSD_CHUNK = 128
SSD_CONV_CH = 2048
MLA_HEADS = 8
MLA_NOPE = 128
MLA_ROPE = 32
MLA_V = 128
MLA_Q_RANK = 768
MLA_KV_RANK = 256
MLA_SCALE = (MLA_NOPE + MLA_ROPE) ** -0.5
ROPE_THETA = 10000.0
N_EXPERTS = 64
TOP_K = 8
EXPERT_DIM = 256
SHARED_DIM = 1024
N_ROUTE_GROUPS = 8
TOPK_ROUTE_GROUPS = 4
ROUTED_SCALE = 2.5
PAGE_SIZE = 128
EPS = 1e-6

LANES = 128
ROW_GROUP = 8
VMEM_LIMIT = 52 * 1024 * 1024
HI = lax.Precision.HIGHEST

SEG = dict(f_q=(0, 1024), g_v=(1024, 1024), g_o=(2048, 1024), s_z=(3072, 1024), s_xbc=(4096, 2048),
           g_q=(6144, 512), g_k=(6656, 512), m_ckv=(7168, 256), f_k=(7424, 128), f_v=(7552, 128),
           m_cq=(7680, 768), small=(8448, 128), pad=(8576, 128), br_gate=(8704, 16384))
N_IN_R = 8704 + 16384
SM_KR, SM_FF, SM_GA, SM_DT = 0, 32, 40, 56


def _cblk(name, width):
    off = SEG[name][0]
    assert off % width == 0
    return off // width


def _cp(sem, vmem=VMEM_LIMIT):
    return pltpu.CompilerParams(dimension_semantics=sem, vmem_limit_bytes=vmem)


def _silu(x):
    return x * jax.nn.sigmoid(x)


def _tile(x, reps, axis):
    return x if reps == 1 else jnp.concatenate([x] * reps, axis=axis)


def _adaln_kernel(c_ref, w_ref, b_ref, o_ref):
    a = _silu(c_ref[...]).astype(BF16)
    o_ref[...] = jnp.dot(a, w_ref[...].astype(BF16), preferred_element_type=F32) + b_ref[...]


def adaln(c, w, b, layer, tn=512):
    m, k = c.shape
    n = w.shape[2]
    return pl.pallas_call(
        _adaln_kernel,
        out_shape=jax.ShapeDtypeStruct((m, n), F32),
        grid=(n // tn,),
        in_specs=[pl.BlockSpec((m, k), lambda j: (0, 0)),
                  pl.BlockSpec((None, k, tn), lambda j: (layer, 0, j)),
                  pl.BlockSpec((None, 1, tn), lambda j: (layer, 0, j))],
        out_specs=pl.BlockSpec((m, tn), lambda j: (0, j)),
        compiler_params=_cp(("arbitrary",)),
        name="adaln",
    )(c, w, b.reshape(b.shape[0], 1, n))


def _mod_specs(comp, tile_rows, td, dims, col_axis):
    nb, ns, bs, nl = dims
    assert nl == ROW_GROUP and ns % tile_rows == 0 and (bs * nl) % tile_rows == 0
    gb = tile_rows // ROW_GROUP
    npt = nb * ns // tile_rows

    def col(idx):
        return idx[col_axis] if col_axis is not None else 0

    modp = pl.BlockSpec((None, None, 1, td),
                        lambda *idx: (jnp.minimum(idx[0] * tile_rows // ns, nb - 1), comp, 0, col(idx)))
    mods = pl.BlockSpec((gb, None, 1, td), lambda *idx: (jnp.maximum(idx[0] - npt, 0), comp, 0, col(idx)))
    return modp, mods, npt


def _tile_mod(modp_ref, mods_ref, npt):
    return jnp.where(pl.program_id(0) < npt, modp_ref[...][None], mods_ref[...])


def _norm_mod_kernel(x_ref, g_ref, shp_ref, shs_ref, scp_ref, scs_ref, *o_refs, npt):
    x = x_ref[...]
    gb = x.shape[0]
    y = x * lax.rsqrt(jnp.mean(x * x, axis=-1, keepdims=True) + EPS) * g_ref[...]
    u = y * (1.0 + _tile_mod(scp_ref, scs_ref, npt)) + _tile_mod(shp_ref, shs_ref, npt)
    u = u.reshape(gb * ROW_GROUP, x.shape[-1])
    for o_ref in o_refs:
        o_ref[...] = u.astype(o_ref.dtype)


def _norm_kernel(x_ref, g_ref, o_ref):
    x = x_ref[...]
    o_ref[...] = (x * lax.rsqrt(jnp.mean(x * x, axis=-1, keepdims=True) + EPS) * g_ref[...]).astype(o_ref.dtype)


def norm_mod(x, g, modp, mods, i_shift, i_scale, dims, out_dtypes=(BF16,), gb=32):
    m, d = x.shape
    ng = m // ROW_GROUP
    x3 = x.reshape(ng, ROW_GROUP, d)
    tile_rows = gb * ROW_GROUP
    shp, shs, npt = _mod_specs(i_shift, tile_rows, d, dims, None)
    scp, scs, _ = _mod_specs(i_scale, tile_rows, d, dims, None)
    out_spec = pl.BlockSpec((tile_rows, d), lambda i: (i, 0))
    return pl.pallas_call(
        functools.partial(_norm_mod_kernel, npt=npt),
        out_shape=tuple(jax.ShapeDtypeStruct((m, d), dt) for dt in out_dtypes),
        grid=(ng // gb,),
        in_specs=[pl.BlockSpec((gb, ROW_GROUP, d), lambda i: (i, 0, 0)),
                  pl.BlockSpec((1, d), lambda i: (0, 0)), shp, shs, scp, scs],
        out_specs=tuple(out_spec for _ in out_dtypes),
        compiler_params=_cp(("arbitrary",)),
        name="norm_mod",
    )(x3, g.reshape(1, d), modp, mods, modp, mods)


def rmsnorm_rows(x, g, out_dtype, tm=256):
    m, d = x.shape
    return pl.pallas_call(
        _norm_kernel,
        out_shape=jax.ShapeDtypeStruct((m, d), out_dtype),
        grid=(m // tm,),
        in_specs=[pl.BlockSpec((tm, d), lambda i: (i, 0)), pl.BlockSpec((1, d), lambda i: (0, 0))],
        out_specs=pl.BlockSpec((tm, d), lambda i: (i, 0)),
        compiler_params=_cp(("arbitrary",)),
        name="rmsnorm_rows",
    )(x, g.reshape(1, d))


NT = (((1,), (1,)), ((), ()))
TN = (((0,), (0,)), ((), ()))


def _mm_nt_kernel(a_ref, bt_ref, o_ref):
    o_ref[...] = lax.dot_general(a_ref[...], bt_ref[...], NT, preferred_element_type=F32).astype(o_ref.dtype)


def matmul_nt(a, bt, layer, out_dtype=F32, tm=1024, tn=512):
    m, k = a.shape
    n = bt.shape[1]
    return pl.pallas_call(
        _mm_nt_kernel,
        out_shape=jax.ShapeDtypeStruct((m, n), out_dtype),
        grid=(m // tm, n // tn),
        in_specs=[pl.BlockSpec((tm, k), lambda i, j: (i, 0)),
                  pl.BlockSpec((None, tn, k), lambda i, j: (layer, j, 0))],
        out_specs=pl.BlockSpec((tm, tn), lambda i, j: (i, j)),
        compiler_params=_cp(("arbitrary", "arbitrary")),
        name="matmul_nt",
    )(a, bt)


def _mm_resid_kernel(a_ref, b_ref, h_ref, gp_ref, gs_ref, *rest, npt):
    if len(rest) == 2:
        extra_ref, o_ref = rest
    else:
        extra_ref, (o_ref,) = None, rest
    acc = jnp.dot(a_ref[...], b_ref[...], preferred_element_type=F32)
    if extra_ref is not None:
        acc = acc + extra_ref[...].astype(F32)
    tm, tn = acc.shape
    acc3 = acc.reshape(tm // ROW_GROUP, ROW_GROUP, tn) * _tile_mod(gp_ref, gs_ref, npt)
    o_ref[...] = h_ref[...] + acc3.reshape(tm, tn)


def matmul_resid(a, b, layer, h, modp, mods, i_gate, dims, extra=None, tm=512, tn=512):
    m, k = a.shape
    n = b.shape[2]
    gp, gs, npt = _mod_specs(i_gate, tm, tn, dims, 1)
    in_specs = [pl.BlockSpec((tm, k), lambda i, j: (i, 0)),
                pl.BlockSpec((None, k, tn), lambda i, j: (layer, 0, j)),
                pl.BlockSpec((tm, tn), lambda i, j: (i, j)), gp, gs]
    args = [a, b, h, modp, mods]
    if extra is not None:
        in_specs.append(pl.BlockSpec((tm, tn), lambda i, j: (i, j)))
        args.append(extra)
    return pl.pallas_call(
        functools.partial(_mm_resid_kernel, npt=npt),
        out_shape=jax.ShapeDtypeStruct((m, n), F32),
        grid=(m // tm, n // tn),
        in_specs=in_specs,
        out_specs=pl.BlockSpec((tm, tn), lambda i, j: (i, j)),
        compiler_params=_cp(("arbitrary", "arbitrary")),
        name="matmul_resid",
    )(*args)


def _swiglu_kernel(a_ref, bg_ref, bu_ref, o_ref):
    a = a_ref[...]
    g = jnp.dot(a, bg_ref[...], preferred_element_type=F32)
    u = jnp.dot(a, bu_ref[...], preferred_element_type=F32)
    o_ref[...] = (_silu(g) * u).astype(o_ref.dtype)


def swiglu_up(a, bg, bu, layer, tm=1024, tn=256):
    m, k = a.shape
    n = bg.shape[2]
    wspec = pl.BlockSpec((None, k, tn), lambda i, j: (layer, 0, j))
    return pl.pallas_call(
        _swiglu_kernel,
        out_shape=jax.ShapeDtypeStruct((m, n), BF16),
        grid=(m // tm, n // tn),
        in_specs=[pl.BlockSpec((tm, k), lambda i, j: (i, 0)), wspec, wspec],
        out_specs=pl.BlockSpec((tm, tn), lambda i, j: (i, j)),
        compiler_params=_cp(("arbitrary", "arbitrary")),
        name="swiglu_up",
    )(a, bg, bu)


def _log_sigmoid(x):
    return jnp.minimum(x, 0.0) - jnp.log1p(jnp.exp(-jnp.abs(x)))


def _softplus(x):
    return jnp.maximum(x, 0.0) + jnp.log1p(jnp.exp(-jnp.abs(x)))


def _rope_lanes(x, c, s1, s2):
    return x * c + pltpu.roll(x, LANES - MLA_ROPE // 2, 1) * s1 + pltpu.roll(x, MLA_ROPE // 2, 1) * s2


def _prep_kernel(sm_ref, ckv_ref, cq_ref, brow_ref, rc_ref, rs1_ref, rs2_ref, wa_ref, ba_ref, gkv_ref, gq_ref,
                 sm_o, la_o, lat_o, latb_o, krp_o, cqn_o):
    x = sm_ref[...]
    xb = x + brow_ref[...]
    lane = lax.broadcasted_iota(jnp.int32, x.shape, 1)
    roped = _rope_lanes(x, rc_ref[...], rs1_ref[...], rs2_ref[...])
    out = jnp.where(lane < SM_FF, roped,
                    jnp.where(lane < SM_GA, _log_sigmoid(xb),
                              jnp.where((lane >= SM_DT) & (lane < SM_DT + SSD_HEADS), _softplus(xb), x)))
    sm_o[...] = out
    krp_o[...] = jnp.where(lane < MLA_ROPE, roped, 0.0).astype(BF16)
    ga = jnp.dot(x.astype(BF16), wa_ref[...], preferred_element_type=F32) + ba_ref[...]
    la_o[...] = _log_sigmoid(ga) * (1.0 / GLA_TAU)
    ckv = ckv_ref[...]
    lat = ckv * lax.rsqrt(jnp.mean(ckv * ckv, axis=-1, keepdims=True) + EPS) * gkv_ref[...]
    lat_o[...] = lat
    latb_o[...] = lat.astype(BF16)
    cq = cq_ref[...]
    cqn_o[...] = (cq * lax.rsqrt(jnp.mean(cq * cq, axis=-1, keepdims=True) + EPS) * gq_ref[...]).astype(BF16)


def prep_tokens(y, brow, rc, rs1, rs2, wa_pad, ba, gkv, gq, tm=512):
    m = y.shape[0]
    row = lambda w: pl.BlockSpec((1, w), lambda i: (0, 0))
    tok = lambda w: pl.BlockSpec((tm, w), lambda i: (i, 0))
    return pl.pallas_call(
        _prep_kernel,
        out_shape=(jax.ShapeDtypeStruct((m, LANES), F32), jax.ShapeDtypeStruct((m, 512), F32),
                   jax.ShapeDtypeStruct((m, MLA_KV_RANK), F32), jax.ShapeDtypeStruct((m, MLA_KV_RANK), BF16),
                   jax.ShapeDtypeStruct((m, LANES), BF16), jax.ShapeDtypeStruct((m, MLA_Q_RANK), BF16)),
        grid=(m // tm,),
        in_specs=[pl.BlockSpec((tm, LANES), lambda i: (i, _cblk("small", LANES))),
                  pl.BlockSpec((tm, MLA_KV_RANK), lambda i: (i, _cblk("m_ckv", MLA_KV_RANK))),
                  pl.BlockSpec((tm, MLA_Q_RANK), lambda i: (i, _cblk("m_cq", MLA_Q_RANK))),
                  row(LANES), tok(LANES), tok(LANES), tok(LANES),
                  pl.BlockSpec((LANES, 512), lambda i: (0, 0)), row(512), row(MLA_KV_RANK), row(MLA_Q_RANK)],
        out_specs=(tok(LANES), tok(512), tok(MLA_KV_RANK), tok(MLA_KV_RANK), tok(LANES), tok(MLA_Q_RANK)),
        compiler_params=_cp(("arbitrary",)),
        name="prep_tokens",
    )(y, y, y, brow, rc, rs1, rs2, wa_pad, ba.reshape(1, 512), gkv.reshape(1, -1), gq.reshape(1, -1))


def _mla_q_kernel(cqn_ref, wn_ref, wr_ref, wuk_ref, rc_ref, rs1_ref, rs2_ref, ql_o, qr_o):
    cq = cqn_ref[...]
    qn = jnp.dot(cq, wn_ref[...], preferred_element_type=F32)
    qr = jnp.dot(cq, wr_ref[...], preferred_element_type=F32)
    c, s1, s2 = rc_ref[...], rs1_ref[...], rs2_ref[...]
    for h in range(MLA_HEADS):
        sl = slice(h * LANES, (h + 1) * LANES)
        qr_o[h] = _rope_lanes(qr[:, sl], c, s1, s2).astype(BF16)
        ql_o[h] = jnp.dot(qn[:, sl].astype(BF16), wuk_ref[h], preferred_element_type=F32).astype(BF16)


def mla_q(cqn, wn, wr, wuk_t, rc, rs1, rs2, tm=512):
    m = cqn.shape[0]
    tok = lambda w: pl.BlockSpec((tm, w), lambda i: (i, 0))
    return pl.pallas_call(
        _mla_q_kernel,
        out_shape=(jax.ShapeDtypeStruct((MLA_HEADS, m, MLA_KV_RANK), BF16),
                   jax.ShapeDtypeStruct((MLA_HEADS, m, LANES), BF16)),
        grid=(m // tm,),
        in_specs=[tok(MLA_Q_RANK),
                  pl.BlockSpec((MLA_Q_RANK, MLA_HEADS * LANES), lambda i: (0, 0)),
                  pl.BlockSpec((MLA_Q_RANK, MLA_HEADS * LANES), lambda i: (0, 0)),
                  pl.BlockSpec((MLA_HEADS, MLA_NOPE, MLA_KV_RANK), lambda i: (0, 0, 0)),
                  tok(LANES), tok(LANES), tok(LANES)],
        out_specs=(pl.BlockSpec((MLA_HEADS, tm, MLA_KV_RANK), lambda i: (0, i, 0)),
                   pl.BlockSpec((MLA_HEADS, tm, LANES), lambda i: (0, i, 0))),
        compiler_params=_cp(("arbitrary",)),
        name="mla_q",
    )(cqn, wn, wr, wuk_t, rc, rs1, rs2)


NEG = -1e30


def _online_softmax_step(s, v, m_ref, l_ref, acc_ref):
    tk, dv = s.shape[1], acc_ref.shape[1]
    m_prev = m_ref[...]
    m_new = jnp.maximum(m_prev, jnp.max(s, axis=-1, keepdims=True))
    alpha = jnp.exp(m_prev - m_new)
    p = jnp.exp(s - _tile(m_new, tk // LANES, axis=1))
    l_ref[...] = alpha * l_ref[...] + jnp.sum(p, axis=-1, keepdims=True)
    m_ref[...] = m_new
    pv = jnp.dot(p.astype(BF16), v, preferred_element_type=F32)
    acc_ref[...] = acc_ref[...] * _tile(alpha, dv // LANES, axis=1) + pv


def _causal_mask(s3, qi, ki, tq, tk):
    qpos = qi * tq + lax.broadcasted_iota(jnp.int32, (1, tq, tk), 1)
    kpos = ki * tk + lax.broadcasted_iota(jnp.int32, (1, tq, tk), 2)
    return jnp.where(kpos <= qpos, s3, NEG)


def _init_softmax(m_ref, l_ref, acc_ref):
    m_ref[...] = jnp.full(m_ref.shape, NEG, F32)
    l_ref[...] = jnp.zeros(l_ref.shape, F32)
    acc_ref[...] = jnp.zeros(acc_ref.shape, F32)


def _fox_prompt_kernel(q_ref, k_ref, v_ref, ck_ref, o_ref, qs_ref, m_ref, l_ref, acc_ref, *, tq, tk):
    qi, ki, nk = pl.program_id(1), pl.program_id(2), pl.num_programs(2)
    hd = FOX_HEAD_DIM

    @pl.when(ki == 0)
    def _():
        _init_softmax(m_ref, l_ref, acc_ref)
        for h in range(FOX_HEADS):
            qs_ref[h * tq:(h + 1) * tq, :] = q_ref[:, h * hd:(h + 1) * hd].astype(BF16)

    @pl.when(ki * tk <= qi * tq + tq - 1)
    def _():
        s = lax.dot_general(qs_ref[...], k_ref[...].astype(BF16), NT, preferred_element_type=F32) * (hd ** -0.5)
        s3 = s.reshape(FOX_HEADS, tq, tk) - ck_ref[0][:, None, :]
        s3 = _causal_mask(s3, qi, ki, tq, tk)
        _online_softmax_step(s3.reshape(FOX_HEADS * tq, tk), v_ref[...].astype(BF16), m_ref, l_ref, acc_ref)

    @pl.when(ki == nk - 1)
    def _():
        o = acc_ref[...] / l_ref[...]
        for h in range(FOX_HEADS):
            o_ref[:, h * hd:(h + 1) * hd] = o[h * tq:(h + 1) * tq].astype(o_ref.dtype)


def fox_prompt(y, cum_t, batch, seq, tq=256, tk=512):
    nq, nk = seq // tq, seq // tk
    kmax = lambda qi, ki: jnp.minimum(ki, (qi * tq + tq - 1) // tk)
    w = FOX_HEADS * FOX_HEAD_DIM
    rows = FOX_HEADS * tq
    return pl.pallas_call(
        functools.partial(_fox_prompt_kernel, tq=tq, tk=tk),
        out_shape=jax.ShapeDtypeStruct((batch * seq, w), BF16),
        grid=(batch, nq, nk),
        in_specs=[pl.BlockSpec((tq, w), lambda b, qi, ki: (b * nq + qi, _cblk("f_q", w))),
                  pl.BlockSpec((tk, LANES), lambda b, qi, ki: (b * nk + kmax(qi, ki), _cblk("f_k", LANES))),
                  pl.BlockSpec((tk, LANES), lambda b, qi, ki: (b * nk + kmax(qi, ki), _cblk("f_v", LANES))),
                  pl.BlockSpec((1, FOX_HEADS, tk), lambda b, qi, ki: (b, 0, kmax(qi, ki)))],
        out_specs=pl.BlockSpec((tq, w), lambda b, qi, ki: (b * nq + qi, 0)),
        scratch_shapes=[pltpu.VMEM((rows, FOX_HEAD_DIM), BF16), pltpu.VMEM((rows, LANES), F32),
                        pltpu.VMEM((rows, LANES), F32), pltpu.VMEM((rows, FOX_HEAD_DIM), F32)],
        compiler_params=_cp(("arbitrary", "arbitrary", "arbitrary")),
        name="fox_prompt",
    )(y, y, y, cum_t)


def _mla_prompt_kernel(ql_ref, qr_ref, lat_ref, kr_ref, wuv_ref, o_ref, m_ref, l_ref, acc_ref, *, tq, tk):
    qi, ki, nk = pl.program_id(1), pl.program_id(2), pl.num_programs(2)
    nh = MLA_HEADS

    @pl.when(ki == 0)
    def _():
        _init_softmax(m_ref, l_ref, acc_ref)

    @pl.when(ki * tk <= qi * tq + tq - 1)
    def _():
        lat = lat_ref[...]
        s = lax.dot_general(ql_ref[...].reshape(nh * tq, MLA_KV_RANK), lat, NT, preferred_element_type=F32)
        s = s + lax.dot_general(qr_ref[...].reshape(nh * tq, LANES), kr_ref[...], NT, preferred_element_type=F32)
        s3 = _causal_mask((s * MLA_SCALE).reshape(nh, tq, tk), qi, ki, tq, tk)
        _online_softmax_step(s3.reshape(nh * tq, tk), lat, m_ref, l_ref, acc_ref)

    @pl.when(ki == nk - 1)
    def _():
        o = acc_ref[...] / _tile(l_ref[...], MLA_KV_RANK // LANES, axis=1)
        for h in range(nh):
            oh = jnp.dot(o[h * tq:(h + 1) * tq].astype(BF16), wuv_ref[h], preferred_element_type=F32)
            o_ref[:, h * MLA_V:(h + 1) * MLA_V] = oh.astype(o_ref.dtype)


def mla_prompt(ql, qr, latb, krp, wuv_t, batch, seq, tq=256, tk=512):
    nq, nk = seq // tq, seq // tk
    kmax = lambda qi, ki: jnp.minimum(ki, (qi * tq + tq - 1) // tk)
    rows = MLA_HEADS * tq
    return pl.pallas_call(
        functools.partial(_mla_prompt_kernel, tq=tq, tk=tk),
        out_shape=jax.ShapeDtypeStruct((batch * seq, MLA_HEADS * MLA_V), BF16),
        grid=(batch, nq, nk),
        in_specs=[pl.BlockSpec((MLA_HEADS, tq, MLA_KV_RANK), lambda b, qi, ki: (0, b * nq + qi, 0)),
                  pl.BlockSpec((MLA_HEADS, tq, LANES), lambda b, qi, ki: (0, b * nq + qi, 0)),
                  pl.BlockSpec((tk, MLA_KV_RANK), lambda b, qi, ki: (b * nk + kmax(qi, ki), 0)),
                  pl.BlockSpec((tk, LANES), lambda b, qi, ki: (b * nk + kmax(qi, ki), 0)),
                  pl.BlockSpec((MLA_HEADS, MLA_KV_RANK, MLA_V), lambda b, qi, ki: (0, 0, 0))],
        out_specs=pl.BlockSpec((tq, MLA_HEADS * MLA_V), lambda b, qi, ki: (b * nq + qi, 0)),
        scratch_shapes=[pltpu.VMEM((rows, LANES), F32), pltpu.VMEM((rows, LANES), F32),
                        pltpu.VMEM((rows, MLA_KV_RANK), F32)],
        compiler_params=_cp(("arbitrary", "arbitrary", "arbitrary")),
        name="mla_prompt",
    )(ql, qr, latb, krp, wuv_t)


PAGES_PER_CHUNK = 16
SLOTS = 2


def _run_paged_chunks(seq, n_seq, n_chunks, chunk_copies, consume):
    assert n_chunks % SLOTS == 0

    @pl.when(seq == 0)
    def _():
        for cp in chunk_copies(seq, n_chunks - 1, 0):
            cp.start()

    for k in range(n_chunks):
        chunk, slot = n_chunks - 1 - k, k % SLOTS
        if k + 1 < n_chunks:
            for cp in chunk_copies(seq, chunk - 1, 1 - slot):
                cp.start()
        else:
            @pl.when(seq + 1 < n_seq)
            def _():
                for cp in chunk_copies(seq + 1, n_chunks - 1, 1 - slot):
                    cp.start()
        for cp in chunk_copies(seq, chunk, slot):
            cp.wait()
        consume(slot)


def _fox_sample_kernel(pt_ref, q_ref, kn_ref, vn_ref, bn_ref, um_ref, lm_ref, kpool, vpool, lfpool, o_ref,
                       kbuf, vbuf, lfbuf, sems, m_ref, l_ref, acc_ref, carry_ref, *, layer, n_pages):
    seq, n_seq = pl.program_id(0), pl.num_programs(0)
    ppc, nh = PAGES_PER_CHUNK, FOX_HEADS
    rows = q_ref.shape[1]
    nt = rows // nh
    scale = FOX_HEAD_DIM ** -0.5

    def chunk_copies(sq, chunk, slot):
        cps = []
        for pg in range(ppc):
            page = pt_ref[sq, chunk * ppc + pg]
            keys = pl.ds(pg * PAGE_SIZE, PAGE_SIZE)
            cps.append(pltpu.make_async_copy(kpool.at[layer, page, :, 0, :], kbuf.at[slot, keys, :], sems.at[0, slot]))
            cps.append(pltpu.make_async_copy(vpool.at[layer, page, :, 0, :], vbuf.at[slot, keys, :], sems.at[1, slot]))
            cps.append(pltpu.make_async_copy(lfpool.at[layer, page], lfbuf.at[slot, pl.ds(pg * nh, nh), :],
                                             sems.at[2, slot]))
        return cps

    _init_softmax(m_ref, l_ref, acc_ref)
    carry_ref[...] = jnp.zeros(carry_ref.shape, F32)
    q = q_ref[0]

    def consume(slot):
        s = lax.dot_general(q, kbuf[slot].astype(BF16), NT, preferred_element_type=F32) * scale
        lf = lfbuf[slot]
        later_in_page = jnp.dot(lf, um_ref[...], precision=HI, preferred_element_type=F32)
        page_tot = jnp.broadcast_to(jnp.sum(lf, axis=-1, keepdims=True), lf.shape)
        later_pages = jnp.dot(lm_ref[...], page_tot, precision=HI, preferred_element_type=F32)
        carry = carry_ref[...]
        bias2 = later_in_page + later_pages + _tile(carry, ppc, axis=0)
        bias = jnp.concatenate([bias2[pg * nh:(pg + 1) * nh] for pg in range(ppc)], axis=1)
        s3 = s.reshape(nh, nt, ppc * PAGE_SIZE) + bias[:, None, :]
        _online_softmax_step(s3.reshape(rows, ppc * PAGE_SIZE), vbuf[slot].astype(BF16), m_ref, l_ref, acc_ref)
        carry_ref[...] = carry + sum(page_tot[pg * nh:(pg + 1) * nh] for pg in range(ppc))

    _run_paged_chunks(seq, n_seq, n_pages // ppc, chunk_copies, consume)
    sn = lax.dot_general(q, kn_ref[0], NT, preferred_element_type=F32) * scale + bn_ref[0]
    _online_softmax_step(sn, vn_ref[0], m_ref, l_ref, acc_ref)
    o_ref[0] = acc_ref[...] / l_ref[...]


def _later_masks():
    pos = np.arange(PAGE_SIZE)
    um = (pos[:, None] > pos[None, :]).astype(np.float32)
    r = np.arange(PAGES_PER_CHUNK * FOX_HEADS)
    pg, hd = r // FOX_HEADS, r % FOX_HEADS
    lm = ((hd[:, None] == hd[None, :]) & (pg[None, :] > pg[:, None])).astype(np.float32)
    return jnp.asarray(um), jnp.asarray(lm)


def fox_sample(page_table, q, knew, vnew, bnew, kpool, vpool, lfpool_t, layer):
    b, rows, hd = q.shape
    n_pages = page_table.shape[1]
    ppc = PAGES_PER_CHUNK
    assert ppc * FOX_HEADS == LANES and n_pages % (ppc * SLOTS) == 0
    um, lm = _later_masks()
    per_seq = lambda r, c: pl.BlockSpec((1, r, c), lambda i, pt: (i, 0, 0))
    const = lambda r, c: pl.BlockSpec((r, c), lambda i, pt: (0, 0))
    hbm = pl.BlockSpec(memory_space=pl.ANY)
    grid_spec = pltpu.PrefetchScalarGridSpec(
        num_scalar_prefetch=1,
        grid=(b,),
        in_specs=[per_seq(rows, hd), per_seq(LANES, hd), per_seq(LANES, hd), per_seq(rows, LANES),
                  const(PAGE_SIZE, PAGE_SIZE), const(LANES, LANES), hbm, hbm, hbm],
        out_specs=per_seq(rows, hd),
        scratch_shapes=[pltpu.VMEM((SLOTS, ppc * PAGE_SIZE, hd), F32), pltpu.VMEM((SLOTS, ppc * PAGE_SIZE, hd), F32),
                        pltpu.VMEM((SLOTS, ppc * FOX_HEADS, PAGE_SIZE), F32), pltpu.SemaphoreType.DMA((3, SLOTS)),
                        pltpu.VMEM((rows, LANES), F32), pltpu.VMEM((rows, LANES), F32), pltpu.VMEM((rows, hd), F32),
                        pltpu.VMEM((FOX_HEADS, LANES), F32)],
    )
    return pl.pallas_call(
        functools.partial(_fox_sample_kernel, layer=layer, n_pages=n_pages),
        out_shape=jax.ShapeDtypeStruct((b, rows, hd), F32),
        grid_spec=grid_spec,
        compiler_params=_cp(("arbitrary",)),
        name="fox_sample",
    )(page_table, q, knew, vnew, bnew, um, lm, kpool, vpool, lfpool_t)


def _mla_sample_kernel(pt_ref, ql_ref, qr_ref, latn_ref, krn_ref, bn_ref, latpool, krpool, o_ref,
                       latbuf, krbuf, sems, m_ref, l_ref, acc_ref, *, layer, n_pages):
    seq, n_seq = pl.program_id(0), pl.num_programs(0)
    ppc = PAGES_PER_CHUNK

    def chunk_copies(sq, chunk, slot):
        cps = []
        for pg in range(ppc):
            page = pt_ref[sq, chunk * ppc + pg]
            keys = pl.ds(pg * PAGE_SIZE, PAGE_SIZE)
            cps.append(pltpu.make_async_copy(latpool.at[layer, page], latbuf.at[slot, keys, :], sems.at[0, slot]))
            cps.append(pltpu.make_async_copy(krpool.at[layer, page], krbuf.at[slot, :, keys], sems.at[1, slot]))
        return cps

    _init_softmax(m_ref, l_ref, acc_ref)
    ql, qr = ql_ref[0], qr_ref[0][:, :MLA_ROPE]

    def consume(slot):
        lat = latbuf[slot].astype(BF16)
        s = lax.dot_general(ql, lat, NT, preferred_element_type=F32)
        s = s + jnp.dot(qr, krbuf[slot].astype(BF16), preferred_element_type=F32)
        _online_softmax_step(s * MLA_SCALE, lat, m_ref, l_ref, acc_ref)

    _run_paged_chunks(seq, n_seq, n_pages // ppc, chunk_copies, consume)
    latn = latn_ref[0]
    sn = lax.dot_general(ql, latn, NT, preferred_element_type=F32)
    sn = sn + lax.dot_general(qr, krn_ref[0], NT, preferred_element_type=F32)
    _online_softmax_step(sn * MLA_SCALE + bn_ref[0], latn, m_ref, l_ref, acc_ref)
    o_ref[0] = acc_ref[...] / _tile(l_ref[...], MLA_KV_RANK // LANES, axis=1)


def mla_sample(page_table, ql, qr, latnew, krnew, bnew, latpool, krpool_t, layer):
    b, rows, r = ql.shape
    n_pages = page_table.shape[1]
    ppc = PAGES_PER_CHUNK
    assert n_pages % (ppc * SLOTS) == 0
    per_seq = lambda a, c: pl.BlockSpec((1, a, c), lambda i, pt: (i, 0, 0))
    hbm = pl.BlockSpec(memory_space=pl.ANY)
    grid_spec = pltpu.PrefetchScalarGridSpec(
        num_scalar_prefetch=1,
        grid=(b,),
        in_specs=[per_seq(rows, r), per_seq(rows, LANES), per_seq(LANES, r), per_seq(LANES, MLA_ROPE),
                  per_seq(rows, LANES), hbm, hbm],
        out_specs=per_seq(rows, r),
        scratch_shapes=[pltpu.VMEM((SLOTS, ppc * PAGE_SIZE, r), F32), pltpu.VMEM((SLOTS, MLA_ROPE, ppc * PAGE_SIZE), F32),
                        pltpu.SemaphoreType.DMA((2, SLOTS)),
                        pltpu.VMEM((rows, LANES), F32), pltpu.VMEM((rows, LANES), F32), pltpu.VMEM((rows, r), F32)],
    )
    return pl.pallas_call(
        functools.partial(_mla_sample_kernel, layer=layer, n_pages=n_pages),
        out_shape=jax.ShapeDtypeStruct((b, rows, r), F32),
        grid_spec=grid_spec,
        compiler_params=_cp(("arbitrary",)),
        name="mla_sample",
    )(page_table, ql, qr, latnew, krnew, bnew, latpool, krpool_t)


def _lanes_to_rows(row_vec, n):
    return jnp.broadcast_to(row_vec, (n, n)).T


def _gla_kernel(q_ref, k_ref, v_ref, go_ref, la_ref, s0_ref, gn_ref, o_ref, sfin_ref, s_ref, *, chunk):
    c, nc = pl.program_id(1), pl.num_programs(1)
    dk, dv = GLA_DK, GLA_DV

    @pl.when(c == 0)
    def _():
        s_ref[...] = s0_ref[0]

    row = lax.broadcasted_iota(jnp.int32, (chunk, chunk), 0)
    col = lax.broadcasted_iota(jnp.int32, (chunk, chunk), 1)
    tril = col <= row
    gn = gn_ref[...]
    for h in range(GLA_HEADS):
        ksl, vsl = slice(h * dk, (h + 1) * dk), slice(h * dv, (h + 1) * dv)
        b = jnp.dot(tril.astype(F32), la_ref[:, ksl], precision=HI, preferred_element_type=F32)
        b_last = b[chunk - 1:chunk, :]
        q = q_ref[:, ksl] * (dk ** -0.5)
        k = k_ref[:, ksl]
        q_hat = (q * jnp.exp(b)).astype(BF16)
        k_hat = (k * jnp.exp(-b)).astype(BF16)
        k_tail = (k * jnp.exp(b_last - b)).astype(BF16)
        v = v_ref[:, vsl].astype(BF16)
        a = lax.dot_general(q_hat, k_hat, NT, preferred_element_type=F32)
        a = jnp.where(tril, a, 0.0).astype(BF16)
        s_old = s_ref[h]
        o = (jnp.dot(a, v, preferred_element_type=F32)
             + jnp.dot(q_hat, s_old.astype(BF16), preferred_element_type=F32))
        kv = lax.dot_general(k_tail, v, TN, preferred_element_type=F32)
        decay = _lanes_to_rows(jnp.exp(b_last), dk)
        s_ref[h] = s_old * _tile(decay, dv // dk, axis=1) + kv
        on = o * lax.rsqrt(jnp.mean(o * o, axis=-1, keepdims=True) + EPS) * gn
        o_ref[:, vsl] = (on * _silu(go_ref[:, vsl])).astype(o_ref.dtype)

    @pl.when(c == nc - 1)
    def _():
        sfin_ref[0] = s_ref[...]


def gla(y, log_a, s0, layer, gnorm, row_off, batch, seq, out_dtype):
    chunk = math.gcd(seq, GLA_CHUNK)
    nc = seq // chunk
    r0 = row_off // chunk
    qk_w, v_w = GLA_HEADS * GLA_DK, GLA_HEADS * GLA_DV
    rowblk = lambda w, cb: pl.BlockSpec((chunk, w), lambda b, c: (r0 + b * nc + c, cb))
    st = pl.BlockSpec((1, GLA_HEADS, GLA_DK, GLA_DV), lambda b, c: (b, 0, 0, 0))
    st_in = pl.BlockSpec((None, 1, GLA_HEADS, GLA_DK, GLA_DV), lambda b, c: (layer, b, 0, 0, 0))
    return pl.pallas_call(
        functools.partial(_gla_kernel, chunk=chunk),
        out_shape=(jax.ShapeDtypeStruct((batch * seq, v_w), out_dtype),
                   jax.ShapeDtypeStruct((batch, GLA_HEADS, GLA_DK, GLA_DV), F32)),
        grid=(batch, nc),
        in_specs=[rowblk(qk_w, _cblk("g_q", qk_w)), rowblk(qk_w, _cblk("g_k", qk_w)),
                  rowblk(v_w, _cblk("g_v", v_w)), rowblk(v_w, _cblk("g_o", v_w)),
                  rowblk(qk_w, 0), st_in, pl.BlockSpec((1, GLA_DV), lambda b, c: (0, 0))],
        out_specs=(pl.BlockSpec((chunk, v_w), lambda b, c: (b * nc + c, 0)), st),
        scratch_shapes=[pltpu.VMEM((GLA_HEADS, GLA_DK, GLA_DV), F32)],
        compiler_params=_cp(("arbitrary", "arbitrary")),
        name="gla",
    )(y, y, y, y, log_a, s0, gnorm.reshape(1, GLA_DV))


SSD_PAIRS = SSD_HEADS // 2
CONV_PAD = 8


def _ssd_kernel(xbc_ref, z_ref, dt_ref, dtt_ref, buf_ref, h0_ref, cw_ref, cb_ref, arow_ref, acol_ref,
                dx_ref, e_ref, nw_ref, o_ref, hfin_ref, h_ref, xpad_ref, *, chunk):
    c, nc = pl.program_id(1), pl.num_programs(1)
    n, hw = SSD_STATE, BRANCH_W

    @pl.when(c == 0)
    def _():
        h_ref[...] = h0_ref[0]
        xpad_ref[0:CONV_PAD, :] = buf_ref[0]

    xpad_ref[CONV_PAD:CONV_PAD + chunk, :] = xbc_ref[...]
    conv = cb_ref[...]
    for kk in range(SSD_CONV):
        lo = CONV_PAD - (SSD_CONV - 1) + kk
        conv = conv + xpad_ref[lo:lo + chunk, :] * cw_ref[kk:kk + 1, :]
    xpad_ref[0:CONV_PAD, :] = xpad_ref[chunk:chunk + CONV_PAD, :]
    xbc = _silu(conv)
    x, bm, cm = xbc[:, :hw], xbc[:, hw:hw + SSD_GROUPS * n], xbc[:, hw + SSD_GROUPS * n:]

    row = lax.broadcasted_iota(jnp.int32, (chunk, chunk), 0)
    col = lax.broadcasted_iota(jnp.int32, (chunk, chunk), 1)
    tril = col <= row
    dt, dtt = dt_ref[...], dtt_ref[0]
    cum = jnp.dot(tril.astype(F32), dt * arow_ref[...], precision=HI, preferred_element_type=F32)
    cum_t = jnp.dot(dtt * acol_ref[...], (row <= col).astype(F32), precision=HI, preferred_element_type=F32)
    expand = e_ref[...]
    cum_x = jnp.dot(cum, expand, precision=HI, preferred_element_type=F32)
    dt_x = jnp.dot(dt, expand, precision=HI, preferred_element_type=F32)
    cum_last_x = cum_x[chunk - 1:chunk, :]
    q_decay_x = jnp.exp(cum_x)
    w_s_x = jnp.exp(cum_last_x - cum_x) * dt_x
    chunk_decay_x = jnp.exp(cum_last_x)
    lane = lax.broadcasted_iota(jnp.int32, (chunk, LANES), 1)
    first_half = lane < SSD_HEAD_DIM
    zs = _silu(z_ref[...])
    gw = hw // SSD_GROUPS

    for g in range(SSD_GROUPS):
        bg = bm[:, g * n:(g + 1) * n].astype(BF16)
        cg = cm[:, g * n:(g + 1) * n].astype(BF16)
        cb = lax.dot_general(cg, bg, NT, preferred_element_type=F32)
        ys = []
        for pp in range(gw // LANES):
            pair = g * (gw // LANES) + pp
            sl = slice(pair * LANES, (pair + 1) * LANES)
            xb = x[:, sl]
            y = dx_ref[:, sl] * xb
            for half in range(2):
                hd = 2 * pair + half
                seg = cum[:, hd:hd + 1] - cum_t[hd:hd + 1, :]
                mm = cb * jnp.exp(jnp.where(tril, seg, NEG)) * dtt[hd:hd + 1, :]
                xh = jnp.where(first_half, xb, 0.0) if half == 0 else jnp.where(first_half, 0.0, xb)
                y = y + jnp.dot(mm.astype(BF16), xh.astype(BF16), preferred_element_type=F32)
            hp = h_ref[pair]
            y = y + lax.dot_general(cg, hp.astype(BF16), NT, preferred_element_type=F32) * q_decay_x[:, sl]
            dstate = lax.dot_general((xb * w_s_x[:, sl]).astype(BF16), bg, TN, preferred_element_type=F32)
            h_ref[pair] = hp * _lanes_to_rows(chunk_decay_x[:, sl], LANES) + dstate
            ys.append(y * zs[:, sl])
        ms = sum(jnp.sum(yy * yy, axis=-1, keepdims=True) for yy in ys) * (1.0 / gw)
        inv = lax.rsqrt(ms + EPS)
        for pp, yy in enumerate(ys):
            sl = slice(g * gw + pp * LANES, g * gw + (pp + 1) * LANES)
            o_ref[:, sl] = (yy * inv * nw_ref[:, sl]).astype(o_ref.dtype)

    @pl.when(c == nc - 1)
    def _():
        hfin_ref[0] = h_ref[...]


def ssd(y, dt, buf, h0, layer, conv_w, conv_b, a_neg, d_skip, norm_w, row_off, batch, seq, out_dtype):
    chunk = math.gcd(seq, SSD_CHUNK)
    nc = seq // chunk
    r0 = row_off // chunk
    hw, ch = BRANCH_W, SSD_CONV_CH
    dt_g = dt[row_off:row_off + batch * seq]
    dtt = dt_g.reshape(batch * nc, chunk, SSD_HEADS).transpose(0, 2, 1)
    bufp = jnp.pad(buf, ((0, 0), (0, 0), (CONV_PAD - (SSD_CONV - 1), 0), (0, 0)))
    hp0 = h0.reshape(h0.shape[0], batch, SSD_PAIRS, LANES, SSD_STATE)
    expand = jnp.repeat(jnp.eye(SSD_HEADS, dtype=F32), SSD_HEAD_DIM, axis=1)
    dx = jnp.repeat(d_skip, SSD_HEAD_DIM).reshape(1, hw)
    rowblk = lambda w, cb, off: pl.BlockSpec((chunk, w), lambda b, c: (off + b * nc + c, cb))
    const = lambda shp: pl.BlockSpec(shp, lambda b, c: (0,) * len(shp))
    st = pl.BlockSpec((1, SSD_PAIRS, LANES, SSD_STATE), lambda b, c: (b, 0, 0, 0))
    st_in = pl.BlockSpec((None, 1, SSD_PAIRS, LANES, SSD_STATE), lambda b, c: (layer, b, 0, 0, 0))
    out, hfin = pl.pallas_call(
        functools.partial(_ssd_kernel, chunk=chunk),
        out_shape=(jax.ShapeDtypeStruct((batch * seq, hw), out_dtype),
                   jax.ShapeDtypeStruct((batch, SSD_PAIRS, LANES, SSD_STATE), F32)),
        grid=(batch, nc),
        in_specs=[rowblk(ch, _cblk("s_xbc", ch), r0), rowblk(hw, _cblk("s_z", hw), r0),
                  rowblk(SSD_HEADS, 0, 0),
                  pl.BlockSpec((1, SSD_HEADS, chunk), lambda b, c: (b * nc + c, 0, 0)),
                  pl.BlockSpec((None, 1, CONV_PAD, ch), lambda b, c: (layer, b, 0, 0)), st_in,
                  const((SSD_CONV, ch)), const((1, ch)), const((1, SSD_HEADS)), const((SSD_HEADS, 1)),
                  const((1, hw)), const((SSD_HEADS, hw)), const((1, hw))],
        out_specs=(pl.BlockSpec((chunk, hw), lambda b, c: (b * nc + c, 0)), st),
        scratch_shapes=[pltpu.VMEM((SSD_PAIRS, LANES, SSD_STATE), F32), pltpu.VMEM((chunk + CONV_PAD, ch), F32)],
        compiler_params=_cp(("arbitrary", "arbitrary")),
        name="ssd",
    )(y, y, dt_g, dtt, bufp, hp0, conv_w, conv_b.reshape(1, ch), a_neg.reshape(1, SSD_HEADS),
      a_neg.reshape(SSD_HEADS, 1), dx, expand, norm_w.reshape(1, hw))
    return out, hfin.reshape(batch, SSD_HEADS, SSD_HEAD_DIM, SSD_STATE)


def _bmm_kernel(a_ref, b_ref, o_ref):
    o_ref[0] = jnp.dot(a_ref[0], b_ref[0], preferred_element_type=F32).astype(o_ref.dtype)


def bmm(a, b, out_dtype):
    nb, m, k = a.shape
    n = b.shape[2]
    return pl.pallas_call(
        _bmm_kernel,
        out_shape=jax.ShapeDtypeStruct((nb, m, n), out_dtype),
        grid=(nb,),
        in_specs=[pl.BlockSpec((1, m, k), lambda i: (i, 0, 0)), pl.BlockSpec((1, k, n), lambda i: (i, 0, 0))],
        out_specs=pl.BlockSpec((1, m, n), lambda i: (i, 0, 0)),
        compiler_params=_cp(("arbitrary",)),
        name="bmm",
    )(a, b)


def _merge_kernel(o0, o1, o2, o3, wb_ref, g0, g1, g2, g3, out_ref):
    acc = None
    for n, (o_ref, g_ref) in enumerate(((o0, g0), (o1, g1), (o2, g2), (o3, g3))):
        t = jax.nn.sigmoid(g_ref[...]) * jnp.dot(o_ref[...], wb_ref[n], preferred_element_type=F32)
        acc = t if acc is None else acc + t
    out_ref[...] = acc.astype(out_ref.dtype)


def merge_branches(branches, wb, layer, y, tm=1024, tn=512):
    m, w = branches[0].shape
    d = wb.shape[3]
    g0 = SEG["br_gate"][0]
    assert g0 % tn == 0 and d % tn == 0
    o_spec = pl.BlockSpec((tm, w), lambda i, j: (i, 0))
    gate = lambda n: pl.BlockSpec((tm, tn), lambda i, j: (i, (g0 + n * d) // tn + j))
    return pl.pallas_call(
        _merge_kernel,
        out_shape=jax.ShapeDtypeStruct((m, d), BF16),
        grid=(m // tm, d // tn),
        in_specs=[o_spec] * N_BRANCH + [pl.BlockSpec((None, N_BRANCH, w, tn), lambda i, j: (layer, 0, 0, j))]
                 + [gate(n) for n in range(N_BRANCH)],
        out_specs=pl.BlockSpec((tm, tn), lambda i, j: (i, j)),
        compiler_params=_cp(("arbitrary", "arbitrary")),
        name="merge_branches",
    )(*branches, wb, y, y, y, y)


def _router_kernel(x_ref, w_ref, b_ref, idx_o, wt_o):
    logits = jnp.dot(x_ref[...], w_ref[...], precision=HI, preferred_element_type=F32)
    s = jax.nn.sigmoid(logits)
    sb = s + b_ref[...]
    lane = lax.broadcasted_iota(jnp.int32, s.shape, 1).astype(F32)
    per_group = N_EXPERTS // N_ROUTE_GROUPS
    big = float(LANES)

    def first_argmax(v):
        m = jnp.max(v, axis=-1, keepdims=True)
        return m, jnp.min(jnp.where(v == m, lane, big), axis=-1, keepdims=True)

    in_group, gscore = [], []
    for g in range(N_ROUTE_GROUPS):
        ing = (lane >= g * per_group) & (lane < (g + 1) * per_group)
        v = jnp.where(ing, sb, NEG)
        m1, i1 = first_argmax(v)
        m2 = jnp.max(jnp.where(lane == i1, NEG, v), axis=-1, keepdims=True)
        in_group.append(ing)
        gscore.append(m1 + m2)
    allowed = jnp.zeros(s.shape, jnp.bool_)
    for g in range(N_ROUTE_GROUPS):
        ahead = jnp.zeros(gscore[g].shape, jnp.int32)
        for g2 in range(N_ROUTE_GROUPS):
            if g2 == g:
                continue
            beats = (gscore[g2] >= gscore[g]) if g2 < g else (gscore[g2] > gscore[g])
            ahead = ahead + beats.astype(jnp.int32)
        allowed = allowed | (in_group[g] & (ahead < TOPK_ROUTE_GROUPS))
    v = jnp.where(allowed, sb, NEG)
    idx_out = jnp.zeros(s.shape, jnp.int32)
    wt_out = jnp.zeros(s.shape, F32)
    wsum = jnp.zeros((s.shape[0], 1), F32)
    for kk in range(TOP_K):
        _, ik = first_argmax(v)
        hit = lane == ik
        wk = jnp.sum(jnp.where(hit, s, 0.0), axis=-1, keepdims=True)
        v = jnp.where(hit, NEG, v)
        idx_out = jnp.where(lane == kk, ik.astype(jnp.int32), idx_out)
        wt_out = jnp.where(lane == kk, wk, wt_out)
        wsum = wsum + wk
    idx_o[...] = idx_out
    wt_o[...] = wt_out / wsum * ROUTED_SCALE


def router(x, w_pad, b_pad, tm=256):
    m, d = x.shape
    return pl.pallas_call(
        _router_kernel,
        out_shape=(jax.ShapeDtypeStruct((m, LANES), jnp.int32), jax.ShapeDtypeStruct((m, LANES), F32)),
        grid=(m // tm,),
        in_specs=[pl.BlockSpec((tm, d), lambda i: (i, 0)), pl.BlockSpec((d, LANES), lambda i: (0, 0)),
                  pl.BlockSpec((1, LANES), lambda i: (0, 0))],
        out_specs=(pl.BlockSpec((tm, LANES), lambda i: (i, 0)), pl.BlockSpec((tm, LANES), lambda i: (i, 0))),
        compiler_params=_cp(("arbitrary",)),
        name="router",
    )(x, w_pad, b_pad)


def _experts_kernel(te_ref, tv_ref, x_ref, rw_ref, wg_ref, wu_ref, wd_ref, o_ref):
    i = pl.program_id(0)

    @pl.when(tv_ref[i] > 0)
    def _():
        x = x_ref[...]
        g = jnp.dot(x, wg_ref[0].astype(BF16), preferred_element_type=F32)
        u = jnp.dot(x, wu_ref[0].astype(BF16), preferred_element_type=F32)
        hid = (_silu(g) * u * rw_ref[...]).astype(BF16)
        o_ref[...] = jnp.dot(hid, wd_ref[0].astype(BF16), preferred_element_type=F32).astype(o_ref.dtype)

    @pl.when(tv_ref[i] == 0)
    def _():
        o_ref[...] = jnp.zeros(o_ref.shape, o_ref.dtype)


def routed_experts(xs, row_w, tile_expert, tile_valid, wg, wu, wd, layer, te):
    r, d = xs.shape
    f = wg.shape[3]
    grid_spec = pltpu.PrefetchScalarGridSpec(
        num_scalar_prefetch=2,
        grid=(r // te,),
        in_specs=[pl.BlockSpec((te, d), lambda i, e, v: (i, 0)),
                  pl.BlockSpec((te, 1), lambda i, e, v: (i, 0)),
                  pl.BlockSpec((None, 1, d, f), lambda i, e, v: (layer, e[i], 0, 0)),
                  pl.BlockSpec((None, 1, d, f), lambda i, e, v: (layer, e[i], 0, 0)),
                  pl.BlockSpec((None, 1, f, d), lambda i, e, v: (layer, e[i], 0, 0))],
        out_specs=pl.BlockSpec((te, d), lambda i, e, v: (i, 0)),
    )
    return pl.pallas_call(
        _experts_kernel,
        out_shape=jax.ShapeDtypeStruct((r, d), BF16),
        grid_spec=grid_spec,
        compiler_params=_cp(("arbitrary",)),
        name="routed_experts",
    )(tile_expert, tile_valid, xs, row_w, wg, wu, wd)


EXPERT_TILE = 256


def moe_dispatch(eidx, wts):
    t, k = eidx.shape
    te = EXPERT_TILE
    n_rows = t * k + N_EXPERTS * te
    n_tiles = n_rows // te
    flat_e = eidx.reshape(-1)
    order = jnp.argsort(flat_e, stable=True).astype(jnp.int32)
    rank_of_pair = jnp.argsort(order).astype(jnp.int32)
    experts = jnp.arange(N_EXPERTS, dtype=jnp.int32)
    counts = jnp.sum((flat_e[:, None] == experts[None, :]).astype(jnp.int32), axis=0)
    padded = (counts + te - 1) // te * te
    pad_end = jnp.cumsum(padded)
    pad_start = pad_end - padded
    seg_start = jnp.cumsum(counts) - counts
    dest_flat = pad_start[flat_e] + rank_of_pair - seg_start[flat_e]
    tile_start = jnp.arange(n_tiles, dtype=jnp.int32) * te
    tile_expert = jnp.minimum(jnp.sum((tile_start[:, None] >= pad_end[None, :]).astype(jnp.int32), axis=1),
                              N_EXPERTS - 1)
    tile_valid = (tile_start < pad_end[-1]).astype(jnp.int32)
    row = jnp.arange(n_rows, dtype=jnp.int32)
    row_e = jnp.repeat(tile_expert, te)
    within = row - pad_start[row_e]
    live = (within < counts[row_e]) & jnp.repeat(tile_valid > 0, te)
    pair = order[jnp.clip(seg_start[row_e] + within, 0, t * k - 1)]
    src_tok = jnp.where(live, pair // k, 0)
    row_w = jnp.where(live, wts.reshape(-1)[pair], 0.0)
    return src_tok, row_w.reshape(n_rows, 1), dest_flat, tile_expert, tile_valid


_ORIG_SPLITS = (("f_q", 1024), ("f_k", 128), ("f_v", 128), ("f_f", 8), ("g_q", 512), ("g_k", 512), ("g_v", 1024),
                ("g_o", 1024), ("g_a", 16), ("s_z", 1024), ("s_xbc", 2048), ("s_dt", 16), ("m_cq", 768),
                ("m_ckv", 256), ("m_kr", 32), ("br_gate", 16384))


def _relayout_w_in(w):
    wt = jnp.swapaxes(w, 1, 2)
    src, off = {}, 0
    for name, width in _ORIG_SPLITS:
        src[name] = wt[:, off:off + width]
        off += width
    nl, _, k = wt.shape
    zeros = lambda n: jnp.zeros((nl, n, k), w.dtype)
    small = jnp.concatenate([src["m_kr"], src["f_f"], src["g_a"], src["s_dt"], zeros(LANES - 72)], axis=1)
    parts = dict(src, small=small, pad=zeros(SEG["pad"][1]))
    order = sorted(SEG, key=lambda n: SEG[n][0])
    return jnp.concatenate([parts[n] for n in order], axis=1).astype(BF16)


def _seg(y, name, lo=None, hi=None):
    off, width = SEG[name]
    return y[lo:hi, off:off + width]


def _rope_tables(pos):
    half = MLA_ROPE // 2
    freqs = ROPE_THETA ** (-jnp.arange(half, dtype=F32) / half)
    ang = pos.astype(F32)[:, None] * freqs[None, :]
    cos, sin = jnp.cos(ang), jnp.sin(ang)
    z = jnp.zeros((pos.shape[0], LANES - MLA_ROPE), F32)
    zh = jnp.zeros((pos.shape[0], half), F32)
    return (jnp.concatenate([cos, cos, z], axis=1), jnp.concatenate([-sin, zh, z], axis=1),
            jnp.concatenate([zh, sin, z], axis=1))


def _new_key_bias(key_bias, n_tok):
    b, nh, _ = key_bias.shape
    t = jnp.arange(n_tok)
    causal = t[None, :] <= t[:, None]
    bias = jnp.where(causal[None, None], key_bias[:, :, None, :], NEG).reshape(b, nh * n_tok, n_tok)
    return jnp.pad(bias, ((0, 0), (0, 0), (0, LANES - n_tok)), constant_values=NEG)


def _pad_keys(x):
    return jnp.pad(x, ((0, 0), (0, LANES - x.shape[1]), (0, 0))).astype(BF16)


def _head_major(x, nb, nt, nh):
    w = x.shape[1] // nh
    return x.reshape(nb, nt, nh, w).transpose(0, 2, 1, 3).reshape(nb, nh * nt, w)


def _token_major(x, nb, nt, nh):
    w = x.shape[2]
    return x.reshape(nb, nh, nt, w).transpose(0, 2, 1, 3).reshape(nb * nt, nh * w)


def _layer(l, h, cpad, ropes, dims, caches, page_table, p, pre):
    nb, ns, bs, nl = dims
    mp, ms = nb * ns, bs * nl
    d = D_MODEL
    rc, rs1, rs2 = ropes
    mod = adaln(cpad, p["w_ada"], p["b_ada"], l).reshape(cpad.shape[0], 6, 1, d)
    modp, mods = mod[:nb], mod[nb:nb + bs]

    (u,) = norm_mod(h, p["norm1"][l], modp, mods, 0, 1, dims)
    y = matmul_nt(u, pre["w_in_t"], l)

    brow = (jnp.zeros((1, LANES), F32).at[0, SM_FF:SM_FF + FOX_HEADS].set(p["fox_fb"][l])
            .at[0, SM_DT:SM_DT + SSD_HEADS].set(p["ssd_dt_bias"][l]))
    wa_pad = jnp.zeros((LANES, GLA_HEADS * GLA_DK), F32).at[SM_GA:SM_GA + GLA_GATE_RANK].set(p["gla_w_a2"][l])
    small, log_a, lat, latb, krp, cqn = prep_tokens(y, brow, rc, rs1, rs2, wa_pad.astype(BF16), p["gla_b_a"][l],
                                                    p["mla_kv_norm"][l], p["mla_q_norm"][l])
    logf = small[:, SM_FF:SM_FF + FOX_HEADS]
    dt = small[:, SM_DT:SM_DT + SSD_HEADS]
    kr = small[:, SM_KR:SM_KR + MLA_ROPE]

    cum_p = jnp.cumsum(logf[:mp].reshape(nb, ns, FOX_HEADS), axis=1)
    o_fox_p = fox_prompt(y, cum_p.transpose(0, 2, 1), nb, ns)
    cum_s = jnp.cumsum(logf[mp:].reshape(bs, nl, FOX_HEADS), axis=1)
    o_fox_s = fox_sample(page_table, _head_major(_seg(y, "f_q", mp), bs, nl, FOX_HEADS).astype(BF16),
                         _pad_keys(_seg(y, "f_k", mp).reshape(bs, nl, FOX_HEAD_DIM)),
                         _pad_keys(_seg(y, "f_v", mp).reshape(bs, nl, FOX_HEAD_DIM)),
                         _new_key_bias(-cum_s.transpose(0, 2, 1), nl),
                         caches["fox_k"], caches["fox_v"], pre["fox_logf_t"], l)
    o_fox = jnp.concatenate([o_fox_p, _token_major(o_fox_s, bs, nl, FOX_HEADS).astype(BF16)], axis=0)

    o_gla_p, s_p = gla(y, log_a, jnp.zeros((1, nb, GLA_HEADS, GLA_DK, GLA_DV), F32), 0, p["gla_norm"][l],
                       0, nb, ns, BF16)
    o_gla_s, s_s = gla(y, log_a, caches["gla"], l, p["gla_norm"][l], mp, bs, nl, F32)
    o_gla = jnp.concatenate([o_gla_p, o_gla_s.astype(BF16)], axis=0)

    a_neg = -jnp.exp(p["ssd_A_log"][l])
    ssd_w = (p["ssd_conv_w"][l], p["ssd_conv_b"][l], a_neg, p["ssd_D"][l], p["ssd_norm"][l])
    o_ssd_p, hs_p = ssd(y, dt, jnp.zeros((1, nb, SSD_CONV - 1, SSD_CONV_CH), F32),
                        jnp.zeros((1, nb, SSD_HEADS, SSD_HEAD_DIM, SSD_STATE), F32), 0, *ssd_w, 0, nb, ns, BF16)
    o_ssd_s, hs_s = ssd(y, dt, caches["conv"], caches["ssd"], l, *ssd_w, mp, bs, nl, F32)
    o_ssd = jnp.concatenate([o_ssd_p, o_ssd_s.astype(BF16)], axis=0)
    xbc_p = _seg(y, "s_xbc", 0, mp).reshape(nb, ns, SSD_CONV_CH)
    xbc_s = jnp.concatenate([caches["conv"][l], _seg(y, "s_xbc", mp).reshape(bs, nl, SSD_CONV_CH)], axis=1)
    conv_p, conv_s = xbc_p[:, ns - (SSD_CONV - 1):], xbc_s[:, nl:]

    w_uq = p["mla_w_uq"][l]
    wn = w_uq[:, :, :MLA_NOPE].reshape(MLA_Q_RANK, MLA_HEADS * MLA_NOPE).astype(BF16)
    wr = jnp.pad(w_uq[:, :, MLA_NOPE:], ((0, 0), (0, 0), (0, LANES - MLA_ROPE)))
    wr = wr.reshape(MLA_Q_RANK, MLA_HEADS * LANES).astype(BF16)
    wuk_t = p["mla_w_uk"][l].transpose(1, 2, 0).astype(BF16)
    wuv_t = p["mla_w_uv"][l].transpose(1, 0, 2).astype(BF16)
    ql, qr = mla_q(cqn, wn, wr, wuk_t, rc, rs1, rs2)
    o_mla_p = mla_prompt(ql, qr, latb, krp, wuv_t, nb, ns)
    hm = lambda x: x[:, mp:].reshape(MLA_HEADS, bs, nl, x.shape[2]).transpose(1, 0, 2, 3).reshape(bs, MLA_HEADS * nl, -1)
    o_lat_s = mla_sample(page_table, hm(ql), hm(qr),
                         _pad_keys(latb[mp:].reshape(bs, nl, MLA_KV_RANK)),
                         _pad_keys(krp[mp:, :MLA_ROPE].reshape(bs, nl, MLA_ROPE)),
                         _new_key_bias(jnp.zeros((bs, MLA_HEADS, nl), F32), nl),
                         caches["mla_lat"], pre["mla_kr_t"], l)
    o_lat_h = o_lat_s.reshape(bs, MLA_HEADS, nl, MLA_KV_RANK).transpose(1, 0, 2, 3).reshape(MLA_HEADS, ms, MLA_KV_RANK)
    o_mla_s = bmm(o_lat_h.astype(BF16), wuv_t, BF16).transpose(1, 0, 2).reshape(ms, MLA_HEADS * MLA_V)
    o_mla = jnp.concatenate([o_mla_p, o_mla_s], axis=0)

    merged = merge_branches((o_fox, o_gla, o_ssd, o_mla), pre["w_branch"], l, y)
    h = matmul_resid(merged, pre["w_out"], l, h, modp, mods, 2, dims)

    u2, u2f = norm_mod(h, p["norm2"][l], modp, mods, 3, 4, dims, out_dtypes=(BF16, F32))
    w_router = jnp.pad(p["w_router"][l], ((0, 0), (0, LANES - N_EXPERTS)))
    b_router = jnp.pad(p["router_bias"][l], (0, LANES - N_EXPERTS)).reshape(1, LANES)
    eidx, wts = router(u2f, w_router, b_router)
    src_tok, row_w, dest_flat, tile_expert, tile_valid = moe_dispatch(eidx[:, :TOP_K], wts[:, :TOP_K])
    ys = routed_experts(u2[src_tok], row_w, tile_expert, tile_valid,
                        p["w_exp_gate"], p["w_exp_up"], p["w_exp_down"], l, EXPERT_TILE)
    routed = ys[dest_flat].reshape(mp + ms, TOP_K, d).astype(F32).sum(axis=1)
    hid = swiglu_up(u2, pre["w_sh_gate"], pre["w_sh_up"], l)
    h = matmul_resid(hid, pre["w_sh_down"], l, h, modp, mods, 5, dims, extra=routed)

    def split(x, tail_p, tail_s):
        return x[:mp].reshape((nb, ns) + tail_p), x[mp:].reshape((bs, nl) + tail_s)

    fk = split(_seg(y, "f_k"), (1, FOX_HEAD_DIM), (1, FOX_HEAD_DIM))
    fv = split(_seg(y, "f_v"), (1, FOX_HEAD_DIM), (1, FOX_HEAD_DIM))
    lf = split(logf, (FOX_HEADS,), (FOX_HEADS,))
    la = split(lat, (MLA_KV_RANK,), (MLA_KV_RANK,))
    kk = split(kr, (MLA_ROPE,), (MLA_ROPE,))
    new_p = (fk[0], fv[0], lf[0], la[0], kk[0], s_p, hs_p, conv_p)
    new_s = (fk[1], fv[1], lf[1], la[1], kk[1], s_s, hs_s, conv_s)
    return h, new_p, new_s


def kernel(x_prompt, x_sample, cache_fox_k, cache_fox_v, cache_fox_logf, cache_mla_latent, cache_mla_krope, state_gla, state_ssd, state_conv, page_table, c_prompt, c_sample, norm1, w_ada, b_ada, w_in, fox_fb, gla_w_a2, gla_b_a, gla_norm, ssd_conv_w, ssd_conv_b, ssd_dt_bias, ssd_A_log, ssd_D, ssd_norm, mla_q_norm, mla_w_uq, mla_kv_norm, mla_w_uk, mla_w_uv, w_branch, w_out, norm2, w_router, router_bias, w_exp_gate, w_exp_up, w_exp_down, w_sh_gate, w_sh_up, w_sh_down, final_norm):
    nb, ns, d = x_prompt.shape
    bs, nl, _ = x_sample.shape
    mp, ms = nb * ns, bs * nl
    depth = w_in.shape[0]
    past_len = page_table.shape[1] * PAGE_SIZE
    p = dict(norm1=norm1, w_ada=w_ada, b_ada=b_ada, w_in=w_in, fox_fb=fox_fb, gla_w_a2=gla_w_a2, gla_b_a=gla_b_a,
             gla_norm=gla_norm, ssd_conv_w=ssd_conv_w, ssd_conv_b=ssd_conv_b, ssd_dt_bias=ssd_dt_bias,
             ssd_A_log=ssd_A_log, ssd_D=ssd_D, ssd_norm=ssd_norm, mla_q_norm=mla_q_norm, mla_w_uq=mla_w_uq,
             mla_kv_norm=mla_kv_norm, mla_w_uk=mla_w_uk, mla_w_uv=mla_w_uv, w_branch=w_branch, w_out=w_out,
             norm2=norm2, w_router=w_router, router_bias=router_bias, w_exp_gate=w_exp_gate, w_exp_up=w_exp_up,
             w_exp_down=w_exp_down, w_sh_gate=w_sh_gate, w_sh_up=w_sh_up, w_sh_down=w_sh_down)
    caches = dict(fox_k=cache_fox_k, fox_v=cache_fox_v, fox_logf=cache_fox_logf, mla_lat=cache_mla_latent,
                  mla_kr=cache_mla_krope, gla=state_gla, ssd=state_ssd, conv=state_conv)

    pre = dict(w_in_t=_relayout_w_in(w_in), w_branch=w_branch.astype(BF16), w_out=w_out.astype(BF16),
               w_sh_gate=w_sh_gate.astype(BF16), w_sh_up=w_sh_up.astype(BF16), w_sh_down=w_sh_down.astype(BF16),
               fox_logf_t=jnp.swapaxes(cache_fox_logf, 2, 3), mla_kr_t=jnp.swapaxes(cache_mla_krope, 2, 3))

    h = jnp.concatenate([x_prompt.reshape(mp, d), x_sample.reshape(ms, d)], axis=0)
    n_cond = nb + bs
    cpad = jnp.pad(jnp.concatenate([c_prompt, c_sample], axis=0), ((0, -n_cond % 16), (0, 0)))
    pos = jnp.concatenate([jnp.tile(jnp.arange(ns), nb), jnp.tile(past_len + jnp.arange(nl), bs)])
    ropes = _rope_tables(pos)

    new_p, new_s = [], []
    for l in range(depth):
        h, st_p, st_s = _layer(l, h, cpad, ropes, (nb, ns, bs, nl), caches, page_table, p, pre)
        new_p.append(st_p)
        new_s.append(st_s)
    yn = rmsnorm_rows(h, final_norm, F32)
    outs_p = [jnp.stack(t) for t in zip(*new_p)]
    outs_s = [jnp.stack(t) for t in zip(*new_s)]
    return (yn[:mp].reshape(nb, ns, d), yn[mp:].reshape(bs, nl, d), *outs_p, *outs_s)
```
